```python
import math
import jax
import jax.numpy as jnp
from jax import lax
import numpy as np

D_MODEL = 1024
BATCH = 2
SEQ = 8192
DEPTH = 1

MIX_WIDTH = D_MODEL
GM_WIDTH = MIX_WIDTH // 2
GM_HEADS = 4
GM_HEAD_DIM = GM_WIDTH // GM_HEADS
CHUNK = 128
DA_WIDTH = MIX_WIDTH - GM_WIDTH
DA_HEADS = 4
DA_VDIM = DA_WIDTH // DA_HEADS
DA_QKDIM = DA_VDIM // 2
LAMBDA_INIT = 0.8 - 0.6 * math.exp(-0.3 * (1 - 1))
Q_BLOCK = 128
IN_COLS = 2 * GM_WIDTH + 3 * DA_WIDTH
PEER_HEADS = 8
N_KEYS = 128
N_EXPERTS = N_KEYS * N_KEYS
PEER_QDIM = 256
PEER_HALF = PEER_QDIM // 2
PEER_TOPK = 16
TOK_BLOCK = 128
ALPHA = (2.0 * DEPTH) ** 0.25
BETA = (8.0 * DEPTH) ** -0.25
LN_EPS = 1e-5
NEG_INF = -1e30

kernel_name = "hybrid_gmlp_diffattn_peer_deepnorm"


def _layer_norm(x, g, b):
    xf = x.astype(jnp.float32)
    mu = jnp.mean(xf, axis=-1, keepdims=True)
    var = jnp.mean(jnp.square(xf - mu), axis=-1, keepdims=True)
    y = (xf - mu) * lax.rsqrt(var + LN_EPS)
    return (y * g.astype(jnp.float32) + b.astype(jnp.float32)).astype(x.dtype)


def _rms_norm_heads(x, g):
    xf = x.astype(jnp.float32)
    y = xf * lax.rsqrt(jnp.mean(jnp.square(xf), axis=-1, keepdims=True) + LN_EPS)
    return (y * g.astype(jnp.float32)[None, :, None, :]).astype(x.dtype)


def _gmlp_chunked(z, norm_g, norm_b, w_s, b_s):
    B, S, _ = z.shape
    nc = S // CHUNK
    u = z[..., :GM_WIDTH]
    v = _layer_norm(z[..., GM_WIDTH:], norm_g, norm_b)
    v = v.reshape(B, nc, CHUNK, GM_HEADS, GM_HEAD_DIM)
    mask = jnp.tril(jnp.ones((CHUNK, CHUNK), dtype=w_s.dtype))
    w_causal = w_s * mask[None]
    mixed = jnp.einsum('hts,bnshc->bnthc', w_causal, v) + b_s.T[None, None, :, :, None]
    out = u.reshape(B, nc, CHUNK, GM_HEADS, GM_HEAD_DIM) * mixed
    return out.reshape(B, S, GM_WIDTH)


def _diff_attention(q1, q2, k1, k2, v, lam):
    B, H, S, dk = q1.shape
    dv = v.shape[-1]
    nb = S // Q_BLOCK
    slopes = 2.0 ** (-(8.0 / H) * jnp.arange(1, H + 1, dtype=jnp.float32))
    kpos = jnp.arange(S, dtype=jnp.int32)
    scale = dk ** -0.5

    def block(args):
        qb1, qb2, start = args
        qpos = start + jnp.arange(Q_BLOCK, dtype=jnp.int32)
        dist = (qpos[:, None] - kpos[None, :]).astype(jnp.float32)
        bias = -slopes[:, None, None] * dist
        causal = dist >= 0

        def probs(qb, k):
            logits = jnp.einsum('bhqd,bhkd->bhqk', qb, k).astype(jnp.float32) * scale + bias
            logits = jnp.where(causal, logits, NEG_INF)
            return jax.nn.softmax(logits, axis=-1)

        a = probs(qb1, k1) - lam * probs(qb2, k2)
        return jnp.einsum('bhqk,bhkd->bhqd', a.astype(v.dtype), v)

    qb1 = q1.reshape(B, H, nb, Q_BLOCK, dk).transpose(2, 0, 1, 3, 4)
    qb2 = q2.reshape(B, H, nb, Q_BLOCK, dk).transpose(2, 0, 1, 3, 4)
    starts = jnp.arange(nb, dtype=jnp.int32) * Q_BLOCK
    out = lax.map(block, (qb1, qb2, starts))
    return out.transpose(1, 2, 0, 3, 4).reshape(B, H, S, dv)


def _peer(x, w_q, keys_a, keys_b, u_tab, v_tab):
    B, S, D = x.shape
    q = (x @ w_q).reshape(B, S, PEER_HEADS, 2, PEER_HALF)
    sa = jnp.einsum('bshd,hkd->bshk', q[..., 0, :], keys_a).astype(jnp.float32)
    sb = jnp.einsum('bshd,hkd->bshk', q[..., 1, :], keys_b).astype(jnp.float32)
    va, ia = lax.top_k(sa, PEER_TOPK)
    vb, ib = lax.top_k(sb, PEER_TOPK)
    cand = (va[..., :, None] + vb[..., None, :]).reshape(B, S, PEER_HEADS, PEER_TOPK * PEER_TOPK)
    s, ci = lax.top_k(cand, PEER_TOPK)
    ea = jnp.take_along_axis(ia, ci // PEER_TOPK, axis=-1)
    eb = jnp.take_along_axis(ib, ci % PEER_TOPK, axis=-1)
    expert = ea * N_KEYS + eb
    g = jax.nn.softmax(s, axis=-1)
    T = B * S
    nblk = T // TOK_BLOCK
    xt = x.reshape(nblk, TOK_BLOCK, D)
    et = expert.reshape(nblk, TOK_BLOCK, PEER_HEADS * PEER_TOPK)
    gt = g.reshape(nblk, TOK_BLOCK, PEER_HEADS * PEER_TOPK).astype(x.dtype)

    def blk(args):
        xb, eidx, gb = args
        ub = u_tab[eidx]
        h = jax.nn.gelu(jnp.einsum('tkd,td->tk', ub, xb), approximate=False)
        return jnp.einsum('tk,tkd->td', gb * h, v_tab[eidx])

    out = lax.map(blk, (xt, et, gt))
    return out.reshape(B, S, D)


def setup_inputs(seed: int = 0) -> dict:
    key = jax.random.key(seed)
    ks = jax.random.split(key, 24)
    f32 = jnp.float32
    nrm = lambda k, shape, s: jax.random.normal(k, shape, f32) * s
    return {
        "x": nrm(ks[0], (BATCH, SEQ, D_MODEL), 1.0),
        "w_in": nrm(ks[1], (D_MODEL, IN_COLS), D_MODEL ** -0.5),
        "gm_norm_g": 1.0 + nrm(ks[2], (GM_WIDTH,), 0.02),
        "gm_norm_b": nrm(ks[3], (GM_WIDTH,), 0.02),
        "gm_w_s": nrm(ks[4], (GM_HEADS, CHUNK, CHUNK), CHUNK ** -0.5),
        "gm_b_s": 1.0 + nrm(ks[5], (GM_HEADS, CHUNK), 0.02),
        "lam_q1": nrm(ks[6], (DA_QKDIM,), 0.1),
        "lam_k1": nrm(ks[7], (DA_QKDIM,), 0.1),
        "lam_q2": nrm(ks[8], (DA_QKDIM,), 0.1),
        "lam_k2": nrm(ks[9], (DA_QKDIM,), 0.1),
        "da_norm_g": 1.0 + nrm(ks[10], (DA_HEADS, DA_VDIM), 0.02),
        "w_o": nrm(ks[11], (MIX_WIDTH, D_MODEL), BETA * MIX_WIDTH ** -0.5),
        "ln1_g": 1.0 + nrm(ks[12], (D_MODEL,), 0.02),
        "ln1_b": nrm(ks[13], (D_MODEL,), 0.02),
        "peer_w_q": nrm(ks[14], (D_MODEL, PEER_HEADS * PEER_QDIM), D_MODEL ** -0.5),
        "peer_keys_a": nrm(ks[15], (PEER_HEADS, N_KEYS, PEER_HALF), PEER_HALF ** -0.5),
        "peer_keys_b": nrm(ks[16], (PEER_HEADS, N_KEYS, PEER_HALF), PEER_HALF ** -0.5),
        "peer_u": nrm(ks[17], (N_EXPERTS, D_MODEL), D_MODEL ** -0.5),
        "peer_v": nrm(ks[18], (N_EXPERTS, D_MODEL), BETA * PEER_HEADS ** -0.5),
        "ln2_g": 1.0 + nrm(ks[19], (D_MODEL,), 0.02),
        "ln2_b": nrm(ks[20], (D_MODEL,), 0.02),
    }


def reference(x, w_in, gm_norm_g, gm_norm_b, gm_w_s, gm_b_s, lam_q1, lam_k1, lam_q2, lam_k2,
              da_norm_g, w_o, ln1_g, ln1_b, peer_w_q, peer_keys_a, peer_keys_b, peer_u, peer_v,
              ln2_g, ln2_b):
    B, S, D = x.shape
    for _ in range(DEPTH):
        h = x @ w_in
        z = jax.nn.gelu(h[..., :2 * GM_WIDTH], approximate=False)
        y_gm = _gmlp_chunked(z, gm_norm_g, gm_norm_b, gm_w_s, gm_b_s)
        o = 2 * GM_WIDTH
        q = h[..., o:o + DA_WIDTH].reshape(B, S, DA_HEADS, 2, DA_QKDIM).transpose(0, 2, 1, 3, 4)
        k = h[..., o + DA_WIDTH:o + 2 * DA_WIDTH].reshape(B, S, DA_HEADS, 2, DA_QKDIM).transpose(0, 2, 1, 3, 4)
        v = h[..., o + 2 * DA_WIDTH:o + 3 * DA_WIDTH].reshape(B, S, DA_HEADS, DA_VDIM).transpose(0, 2, 1, 3)
        lam = (jnp.exp(jnp.sum(lam_q1.astype(jnp.float32) * lam_k1.astype(jnp.float32)))
               - jnp.exp(jnp.sum(lam_q2.astype(jnp.float32) * lam_k2.astype(jnp.float32)))
               + LAMBDA_INIT)
        att = _diff_attention(q[..., 0, :], q[..., 1, :], k[..., 0, :], k[..., 1, :], v, lam)
        att = _rms_norm_heads(att, da_norm_g) * (1.0 - LAMBDA_INIT)
        y_da = att.transpose(0, 2, 1, 3).reshape(B, S, DA_WIDTH)
        mix = jnp.concatenate([y_gm, y_da], axis=-1) @ w_o
        x = _layer_norm(ALPHA * x + mix, ln1_g, ln1_b)
        ffn = _peer(x, peer_w_q, peer_keys_a, peer_keys_b, peer_u, peer_v)
        x = _layer_norm(ALPHA * x + ffn, ln2_g, ln2_b)
    return x
```

```python
import functools
import math

import jax
import jax.numpy as jnp
import numpy as np
from jax import lax
from jax.experimental import pallas as pl
from jax.experimental.pallas import tpu as pltpu

F32 = jnp.float32
BF16 = jnp.bfloat16

GM_HEADS = 4
CHUNK = 128
DA_HEADS = 4
LAMBDA_INIT = 0.8 - 0.6 * math.exp(-0.3 * (1 - 1))
PEER_HEADS = 8
N_KEYS = 128
PEER_TOPK = 16
DEPTH = 1
ALPHA = (2.0 * DEPTH) ** 0.25
LN_EPS = 1e-5
NEG_BIG = -1e30
LOG2E = 1.4426950408889634
NOT_RANKED = 64.0

LANES = 128
VMEM_LIMIT = 48 * 1024 * 1024
TM_PROJ = 512
TQ = 256
TK = 512
TL_SEL = 512
TT_DENSE = 1024
SUB_DENSE = 256
EB_DENSE = 256


def _gelu_exact(x):
    return 0.5 * x * (1.0 + lax.erf(x * (1.0 / math.sqrt(2.0))))


def _layer_norm_rows(x, g, b):
    mu = jnp.mean(x, axis=-1, keepdims=True)
    xc = x - mu
    var = jnp.mean(xc * xc, axis=-1, keepdims=True)
    return xc * lax.rsqrt(var + LN_EPS) * g + b


def _nt_dot(a, b):
    return lax.dot_general(a, b, (((1,), (1,)), ((), ())), preferred_element_type=F32)


def _proj_gmlp_kernel(x_ref, w_ref, g_ref, b_ref, ws_ref, bs_ref, ygm_ref, q_ref, k_ref, v_ref,
                      *, gm_width, da_width, q_scale):
    xb = x_ref[...].astype(BF16)
    tm = xb.shape[0]
    hd = gm_width // GM_HEADS
    u = _gelu_exact(jnp.dot(xb, w_ref[:, :gm_width], preferred_element_type=F32))
    z = _gelu_exact(jnp.dot(xb, w_ref[:, gm_width:2 * gm_width], preferred_element_type=F32))
    vn = _layer_norm_rows(z, g_ref[...], b_ref[...]).astype(BF16)
    row = lax.broadcasted_iota(jnp.int32, (CHUNK, CHUNK), 0)
    col = lax.broadcasted_iota(jnp.int32, (CHUNK, CHUNK), 1)
    causal = col <= row
    for hh in range(GM_HEADS):
        w_c = jnp.where(causal, ws_ref[hh], 0.0).astype(BF16)
        for c in range(tm // CHUNK):
            rs = slice(c * CHUNK, (c + 1) * CHUNK)
            cs = slice(hh * hd, (hh + 1) * hd)
            mixed = jnp.dot(w_c, vn[rs, cs], preferred_element_type=F32) + bs_ref[:, cs]
            ygm_ref[rs, cs] = (u[rs, cs] * mixed).astype(BF16)
    o = 2 * gm_width
    q = jnp.dot(xb, w_ref[:, o:o + da_width], preferred_element_type=F32)
    q_ref[...] = (q * q_scale).astype(BF16)
    k_ref[...] = jnp.dot(xb, w_ref[:, o + da_width:o + 2 * da_width],
                         preferred_element_type=F32).astype(BF16)
    v_ref[...] = jnp.dot(xb, w_ref[:, o + 2 * da_width:o + 3 * da_width],
                         preferred_element_type=F32).astype(BF16)


def _proj_gmlp(x2, w_in_b, gm_g, gm_b, w_s, bs_tile, gm_width, da_width, q_scale):
    t, d = x2.shape
    tm = min(TM_PROJ, t)
    cols = w_in_b.shape[1]
    const = lambda shape: pl.BlockSpec(shape, lambda i: (0,) * len(shape))
    row_blk = lambda w: pl.BlockSpec((tm, w), lambda i: (i, 0))
    return pl.pallas_call(
        functools.partial(_proj_gmlp_kernel, gm_width=gm_width, da_width=da_width, q_scale=q_scale),
        grid=(t // tm,),
        in_specs=[row_blk(d), const((d, cols)), const((1, gm_width)), const((1, gm_width)),
                  const((GM_HEADS, CHUNK, CHUNK)), const((CHUNK, gm_width))],
        out_specs=[row_blk(gm_width), row_blk(da_width), row_blk(da_width), row_blk(da_width)],
        out_shape=[jax.ShapeDtypeStruct((t, gm_width), BF16)] + [jax.ShapeDtypeStruct((t, da_width), BF16)] * 3,
        compiler_params=pltpu.CompilerParams(dimension_semantics=("arbitrary",),
                                             vmem_limit_bytes=VMEM_LIMIT),
        name="proj_gmlp",
    )(x2, w_in_b, gm_g, gm_b, w_s, bs_tile)


def _diff_attn_kernel(slope_ref, q_ref, k_ref, v_ref, lq1_ref, lk1_ref, lq2_ref, lk2_ref, g_ref, o_ref,
                      *, tq, tk):
    h = pl.program_id(1)
    qi = pl.program_id(2)
    slope2 = slope_ref[h]
    q = q_ref[...]
    dv = q.shape[1]
    dk = dv // 2
    lane = lax.broadcasted_iota(jnp.int32, q.shape, 1)
    zero = jnp.zeros_like(q)
    q_maps = (jnp.where(lane < dk, q, zero), jnp.where(lane >= dk, q, zero))
    rowi = lax.broadcasted_iota(jnp.int32, (tq, tk), 0)
    coli = lax.broadcasted_iota(jnp.int32, (tq, tk), 1)
    dloc = coli - rowi
    bias_loc = dloc.astype(F32) * slope2

    def block(kj, carry, masked):
        kstart = pl.multiple_of(kj * tk, tk)
        kb = k_ref[pl.ds(kstart, tk), :]
        vb = v_ref[pl.ds(kstart, tk), :]
        rel = kstart - qi * tq
        off = rel.astype(F32) * slope2
        out = []
        for mp in range(2):
            m, l, acc = carry[3 * mp:3 * mp + 3]
            t = _nt_dot(q_maps[mp], kb) + bias_loc
            if masked:
                t = jnp.where(dloc + rel <= 0, t, NEG_BIG)
            m_new = jnp.maximum(m, jnp.max(t, axis=1, keepdims=True) + off)
            p = jnp.exp2(t + (off - m_new))
            alpha = jnp.exp2(m - m_new)
            l = alpha * l + jnp.sum(p, axis=1, keepdims=True)
            acc = alpha * acc + jnp.dot(p.astype(BF16), vb, preferred_element_type=F32)
            out += [m_new, l, acc]
        return tuple(out)

    init = (jnp.full((tq, 1), NEG_BIG, F32), jnp.zeros((tq, 1), F32), jnp.zeros((tq, dv), F32)) * 2
    n_full = (qi * tq) // tk
    carry = lax.fori_loop(0, n_full, lambda kj, c: block(kj, c, False), init)
    m1, l1, a1, m2, l2, a2 = block(n_full, carry, True)

    lam = (jnp.exp(jnp.sum(lq1_ref[...] * lk1_ref[...], axis=1, keepdims=True))
           - jnp.exp(jnp.sum(lq2_ref[...] * lk2_ref[...], axis=1, keepdims=True)) + LAMBDA_INIT)
    att = a1 / l1 - lam * (a2 / l2)
    ms = jnp.mean(att * att, axis=1, keepdims=True)
    y = att * lax.rsqrt(ms + LN_EPS) * g_ref[...]
    o_ref[...] = (y * (1.0 - LAMBDA_INIT)).astype(o_ref.dtype)


def _diff_attn(q3, k3, v3, slopes2, lq1, lk1, lq2, lk2, g3):
    b, s, w = q3.shape
    dv = w // DA_HEADS
    tq = min(TQ, s)
    tk = min(TK, s)
    dk = lq1.shape[1]
    vec = pl.BlockSpec((1, dk), lambda bi, hi, qi: (0, 0))
    return pl.pallas_call(
        functools.partial(_diff_attn_kernel, tq=tq, tk=tk),
        grid=(b, DA_HEADS, s // tq),
        in_specs=[pl.BlockSpec(memory_space=pltpu.SMEM),
                  pl.BlockSpec((None, tq, dv), lambda bi, hi, qi: (bi, qi, hi)),
                  pl.BlockSpec((None, s, dv), lambda bi, hi, qi: (bi, 0, hi)),
                  pl.BlockSpec((None, s, dv), lambda bi, hi, qi: (bi, 0, hi)),
                  vec, vec, vec, vec,
                  pl.BlockSpec((None, 1, dv), lambda bi, hi, qi: (hi, 0, 0))],
        out_specs=pl.BlockSpec((None, tq, dv), lambda bi, hi, qi: (bi, qi, hi)),
        out_shape=jax.ShapeDtypeStruct((b, s, w), BF16),
        compiler_params=pltpu.CompilerParams(dimension_semantics=("arbitrary",) * 3,
                                             vmem_limit_bytes=VMEM_LIMIT),
        name="diff_attn",
    )(slopes2, q3, k3, v3, lq1, lk1, lq2, lk2, g3)


def _mix_ln_peerq_kernel(ygm_ref, yda_ref, x_ref, wo_ref, g_ref, b_ref, wq_ref, ka_ref, kb_ref,
                         x1_ref, x1b_ref, sa_ref, sb_ref):
    gw = ygm_ref.shape[1]
    mix = (jnp.dot(ygm_ref[...], wo_ref[:gw, :], preferred_element_type=F32)
           + jnp.dot(yda_ref[...], wo_ref[gw:, :], preferred_element_type=F32))
    x1 = _layer_norm_rows(ALPHA * x_ref[...] + mix, g_ref[...], b_ref[...])
    x1_ref[...] = x1
    x1b = x1.astype(BF16)
    x1b_ref[...] = x1b
    half = ka_ref.shape[2]
    for hh in range(PEER_HEADS):
        c0 = hh * 2 * half
        qa = jnp.dot(x1b, wq_ref[:, c0:c0 + half], preferred_element_type=F32).astype(BF16)
        qb = jnp.dot(x1b, wq_ref[:, c0 + half:c0 + 2 * half], preferred_element_type=F32).astype(BF16)
        sa_ref[hh] = _nt_dot(ka_ref[hh], qa)
        sb_ref[hh] = _nt_dot(kb_ref[hh], qb)


def _mix_ln_peerq(ygm, yda, x2, wo_b, ln_g, ln_b, wq_b, ka_b, kb_b):
    t, d = x2.shape
    tm = min(TM_PROJ, t)
    gw, dw = ygm.shape[1], yda.shape[1]
    const = lambda shape: pl.BlockSpec(shape, lambda i: (0,) * len(shape))
    row_blk = lambda w: pl.BlockSpec((tm, w), lambda i: (i, 0))
    score_blk = pl.BlockSpec((PEER_HEADS, N_KEYS, tm), lambda i: (0, 0, i))
    score_shape = jax.ShapeDtypeStruct((PEER_HEADS, N_KEYS, t), F32)
    return pl.pallas_call(
        _mix_ln_peerq_kernel,
        grid=(t // tm,),
        in_specs=[row_blk(gw), row_blk(dw), row_blk(d), const(wo_b.shape), const((1, d)), const((1, d)),
                  const(wq_b.shape), const(ka_b.shape), const(kb_b.shape)],
        out_specs=[row_blk(d), row_blk(d), score_blk, score_blk],
        out_shape=[jax.ShapeDtypeStruct((t, d), F32), jax.ShapeDtypeStruct((t, d), BF16),
                   score_shape, score_shape],
        compiler_params=pltpu.CompilerParams(dimension_semantics=("arbitrary",),
                                             vmem_limit_bytes=VMEM_LIMIT),
        name="mix_ln_peerq",
    )(ygm, yda, x2, wo_b, ln_g, ln_b, wq_b, ka_b, kb_b)


def _top16(val, key_iota, vals_ref):
    rank = jnp.full(val.shape, NOT_RANKED, F32)
    for r in range(PEER_TOPK):
        m = jnp.max(val, axis=0, keepdims=True)
        idx = jnp.min(jnp.where(val == m, key_iota, float(N_KEYS)), axis=0, keepdims=True)
        hit = key_iota == idx
        rank = jnp.where(hit, float(r), rank)
        val = jnp.where(hit, -jnp.inf, val)
        vals_ref[r:r + 1, :] = m
    return rank


_PIECES = 10
_J_LIMIT = (8, 8, 8, 5, 4, 3, 2, 2, 2, 8)


def _peer_select_kernel(sa_ref, sb_ref, cnt_ref, wa_ref, rb_ref, wb_ref, va_ref, vb_ref):
    sa = sa_ref[...]
    sb = sb_ref[...]
    tl = sa.shape[1]
    key_iota = lax.broadcasted_iota(jnp.int32, sa.shape, 0).astype(F32)
    ra = _top16(sa, key_iota, va_ref)
    rb = _top16(sb, key_iota, vb_ref)
    va = va_ref[...]
    vb = vb_ref[...]
    j8 = lax.broadcasted_iota(jnp.int32, (8, tl), 0).astype(F32)
    pieces, flats = [], []
    for p in range(_PIECES):
        if p == 0:
            c, f = vb[0:8] + va[0:1], j8
        elif p == 1:
            c, f = vb[8:16] + va[0:1], j8 + 8.0
        elif p == 9:
            c, f = va[8:16] + vb[0:1], 128.0 + 16.0 * j8
        else:
            i = p - 1
            c, f = vb[0:8] + va[i:i + 1], j8 + 16.0 * i
            c = jnp.where(j8 < float(_J_LIMIT[p]), c, -jnp.inf)
        pieces.append(c)
        flats.append(f)
    cand = jnp.concatenate(pieces, axis=0)
    flat = jnp.concatenate(flats, axis=0)
    i16 = lax.broadcasted_iota(jnp.int32, (PEER_TOPK, tl), 0).astype(F32)
    cnt = jnp.zeros((PEER_TOPK, tl), F32)
    s0 = va[0:1] + vb[0:1]
    z = jnp.zeros((1, tl), F32)
    for _ in range(PEER_TOPK):
        m = jnp.max(cand, axis=0, keepdims=True)
        idx = jnp.min(jnp.where(cand == m, flat, 1024.0), axis=0, keepdims=True)
        cand = jnp.where(flat == idx, -jnp.inf, cand)
        cnt = cnt + jnp.where(i16 == jnp.floor(idx * (1.0 / PEER_TOPK)), 1.0, 0.0)
        z = z + jnp.exp(m - s0)
    cnt_key = jnp.zeros(sa.shape, F32)
    for i in range(PEER_TOPK):
        cnt_key = jnp.where(ra == float(i), cnt[i:i + 1], cnt_key)
    cnt_ref[...] = cnt_key
    wa_ref[...] = jnp.exp(sa - va[0:1]) / z
    rb_ref[...] = rb.astype(BF16)
    wb_ref[...] = jnp.exp(sb - vb[0:1]).astype(BF16)


def _peer_select(sa, sb):
    nh, nk, t = sa.shape
    tl = min(TL_SEL, t)
    blk = pl.BlockSpec((None, nk, tl), lambda i, h: (h, 0, i))
    return pl.pallas_call(
        _peer_select_kernel,
        grid=(t // tl, nh),
        in_specs=[blk, blk],
        out_specs=[blk, blk, blk, blk],
        out_shape=[jax.ShapeDtypeStruct(sa.shape, F32), jax.ShapeDtypeStruct(sa.shape, F32),
                   jax.ShapeDtypeStruct(sa.shape, BF16), jax.ShapeDtypeStruct(sa.shape, BF16)],
        scratch_shapes=[pltpu.VMEM((PEER_TOPK, tl), F32), pltpu.VMEM((PEER_TOPK, tl), F32)],
        compiler_params=pltpu.CompilerParams(dimension_semantics=("arbitrary", "arbitrary"),
                                             vmem_limit_bytes=VMEM_LIMIT),
        name="peer_select",
    )(sa, sb)


def _peer_dense_kernel(x1b_ref, u_ref, vt_ref, cnt_ref, wa_ref, rb_ref, wb_ref, o_ref, *, sub):
    j = pl.program_id(1)
    eb = u_ref.shape[0]
    tt = x1b_ref.shape[0]

    @pl.when(j == 0)
    def _():
        o_ref[...] = jnp.zeros_like(o_ref)

    for c in range(tt // sub):
        ts = slice(c * sub, (c + 1) * sub)
        s = _nt_dot(u_ref[...], x1b_ref[ts, :])
        gates = []
        for ai in range(eb // N_KEYS):
            a = j * (eb // N_KEYS) + ai
            g = jnp.zeros((N_KEYS, sub), BF16)
            for hh in range(PEER_HEADS):
                cnt = cnt_ref[hh, pl.ds(a, 1), ts].astype(BF16)
                wa = wa_ref[hh, pl.ds(a, 1), ts].astype(BF16)
                keep = rb_ref[hh, :, ts] < cnt
                g = g + jnp.where(keep, wb_ref[hh, :, ts] * wa, jnp.zeros((), BF16))
            gates.append(g)
        gate = jnp.concatenate(gates, axis=0)
        hact = _gelu_exact(s).astype(BF16) * gate
        o_ref[:, ts] += jnp.dot(vt_ref[...], hact, preferred_element_type=F32)


def _peer_dense(x1b, u_b, vt_b, cnt, wa, rb, wb):
    t, d = x1b.shape
    ne = u_b.shape[0]
    tt = min(TT_DENSE, t)
    sub = min(SUB_DENSE, tt)
    eb = EB_DENSE
    meta = pl.BlockSpec((PEER_HEADS, N_KEYS, tt), lambda i, j: (0, 0, i))
    return pl.pallas_call(
        functools.partial(_peer_dense_kernel, sub=sub),
        grid=(t // tt, ne // eb),
        in_specs=[pl.BlockSpec((tt, d), lambda i, j: (i, 0)),
                  pl.BlockSpec((eb, d), lambda i, j: (j, 0)),
                  pl.BlockSpec((d, eb), lambda i, j: (0, j)),
                  meta, meta, meta, meta],
        out_specs=pl.BlockSpec((d, tt), lambda i, j: (0, i)),
        out_shape=jax.ShapeDtypeStruct((d, t), F32),
        compiler_params=pltpu.CompilerParams(dimension_semantics=("arbitrary", "arbitrary"),
                                             vmem_limit_bytes=VMEM_LIMIT),
        name="peer_dense",
    )(x1b, u_b, vt_b, cnt, wa, rb, wb)


def _ffn_ln_kernel(ffnt_ref, x1_ref, g_ref, b_ref, o_ref):
    ffn = ffnt_ref[...].T
    o_ref[...] = _layer_norm_rows(ALPHA * x1_ref[...] + ffn, g_ref[...], b_ref[...])


def _ffn_ln(ffnt, x1, ln_g, ln_b):
    t, d = x1.shape
    tm = min(TM_PROJ, t)
    const = lambda shape: pl.BlockSpec(shape, lambda i: (0,) * len(shape))
    return pl.pallas_call(
        _ffn_ln_kernel,
        grid=(t // tm,),
        in_specs=[pl.BlockSpec((d, tm), lambda i: (0, i)), pl.BlockSpec((tm, d), lambda i: (i, 0)),
                  const((1, d)), const((1, d))],
        out_specs=pl.BlockSpec((tm, d), lambda i: (i, 0)),
        out_shape=jax.ShapeDtypeStruct((t, d), F32),
        compiler_params=pltpu.CompilerParams(dimension_semantics=("arbitrary",),
                                             vmem_limit_bytes=VMEM_LIMIT),
        name="ffn_ln",
    )(ffnt, x1, ln_g, ln_b)


def kernel(x, w_in, gm_norm_g, gm_norm_b, gm_w_s, gm_b_s, lam_q1, lam_k1, lam_q2, lam_k2, da_norm_g,
           w_o, ln1_g, ln1_b, peer_w_q, peer_keys_a, peer_keys_b, peer_u, peer_v, ln2_g, ln2_b):
    b, s, d = x.shape
    t = b * s
    gm_width = gm_norm_g.shape[0]
    da_width = da_norm_g.shape[0] * da_norm_g.shape[1]
    dk = lam_q1.shape[0]
    assert s % CHUNK == 0 and gm_w_s.shape == (GM_HEADS, CHUNK, CHUNK)
    assert peer_u.shape[0] == N_KEYS * N_KEYS and peer_keys_a.shape[:2] == (PEER_HEADS, N_KEYS)

    row = lambda v: v.reshape(1, -1).astype(F32)
    x2 = x.reshape(t, d)
    bs_tile = jnp.repeat(gm_b_s.T, gm_width // GM_HEADS, axis=1)
    q_scale = (dk ** -0.5) * LOG2E
    ygm, q, k, v = _proj_gmlp(x2, w_in.astype(BF16), row(gm_norm_g), row(gm_norm_b), gm_w_s, bs_tile,
                              gm_width, da_width, q_scale)

    slopes2 = jnp.asarray(2.0 ** (-(8.0 / DA_HEADS) * np.arange(1, DA_HEADS + 1)) * LOG2E, F32)
    yda = _diff_attn(q.reshape(b, s, da_width), k.reshape(b, s, da_width), v.reshape(b, s, da_width),
                     slopes2, row(lam_q1), row(lam_k1), row(lam_q2), row(lam_k2),
                     da_norm_g.reshape(DA_HEADS, 1, -1).astype(F32))

    x1, x1b, sa, sb = _mix_ln_peerq(ygm, yda.reshape(t, da_width), x2, w_o.astype(BF16), row(ln1_g),
                                    row(ln1_b), peer_w_q.astype(BF16), peer_keys_a.astype(BF16),
                                    peer_keys_b.astype(BF16))
    cnt, wa, rb, wb = _peer_select(sa, sb)
    ffnt = _peer_dense(x1b, peer_u.astype(BF16), peer_v.astype(BF16).T, cnt, wa, rb, wb)
    out = _ffn_ln(ffnt, x1, row(ln2_g), row(ln2_b))
    return out.reshape(b, s, d)
```

```python
import functools
import math

import jax
import jax.numpy as jnp
import numpy as np
from jax import lax
from jax.experimental import pallas as pl
from jax.experimental.pallas import tpu as pltpu

F32 = jnp.float32
BF16 = jnp.bfloat16

GM_HEADS = 4
CHUNK = 128
DA_HEADS = 4
LAMBDA_INIT = 0.8 - 0.6 * math.exp(-0.3 * (1 - 1))
PEER_HEADS = 8
N_KEYS = 128
PEER_TOPK = 16
DEPTH = 1
ALPHA = (2.0 * DEPTH) ** 0.25
LN_EPS = 1e-5
NEG_BIG = -1e30
LOG2E = 1.4426950408889634
NOT_RANKED = 64.0

LANES = 128
SUBLANES = 8
VMEM_LIMIT = 48 * 1024 * 1024
TM_PROJ = 512
TQ = 256
TK = 512
TL_SEL = 512
TT_DENSE = 1024
SUB_DENSE = 256
EB_DENSE = 1024


def _gelu_exact(x):
    return 0.5 * x * (1.0 + lax.erf(x * (1.0 / math.sqrt(2.0))))


def _layer_norm_rows(x, g, b):
    mu = jnp.mean(x, axis=-1, keepdims=True)
    xc = x - mu
    var = jnp.mean(xc * xc, axis=-1, keepdims=True)
    return xc * lax.rsqrt(var + LN_EPS) * g + b


def _nt_dot(a, b):
    return lax.dot_general(a, b, (((1,), (1,)), ((), ())), preferred_element_type=F32)


def _proj_gmlp_kernel(x_ref, w_ref, g_ref, b_ref, ws_ref, bs_ref, ygm_ref, q_ref, k_ref, v_ref,
                      *, gm_width, da_width, q_scale):
    xb = x_ref[...].astype(BF16)
    tm = xb.shape[0]
    hd = gm_width // GM_HEADS
    u = _gelu_exact(jnp.dot(xb, w_ref[:, :gm_width], preferred_element_type=F32))
    z = _gelu_exact(jnp.dot(xb, w_ref[:, gm_width:2 * gm_width], preferred_element_type=F32))
    vn = _layer_norm_rows(z, g_ref[...], b_ref[...]).astype(BF16)
    row = lax.broadcasted_iota(jnp.int32, (CHUNK, CHUNK), 0)
    col = lax.broadcasted_iota(jnp.int32, (CHUNK, CHUNK), 1)
    causal = col <= row
    for hh in range(GM_HEADS):
        w_c = jnp.where(causal, ws_ref[hh], 0.0).astype(BF16)
        for c in range(tm // CHUNK):
            rs = slice(c * CHUNK, (c + 1) * CHUNK)
            cs = slice(hh * hd, (hh + 1) * hd)
            mixed = jnp.dot(w_c, vn[rs, cs], preferred_element_type=F32) + bs_ref[:, cs]
            ygm_ref[rs, cs] = (u[rs, cs] * mixed).astype(BF16)
    o = 2 * gm_width
    q = jnp.dot(xb, w_ref[:, o:o + da_width], preferred_element_type=F32)
    q_ref[...] = (q * q_scale).astype(BF16)
    k_ref[...] = jnp.dot(xb, w_ref[:, o + da_width:o + 2 * da_width],
                         preferred_element_type=F32).astype(BF16)
    v_ref[...] = jnp.dot(xb, w_ref[:, o + 2 * da_width:o + 3 * da_width],
                         preferred_element_type=F32).astype(BF16)


def _proj_gmlp(x2, w_in_b, gm_g, gm_b, w_s, bs_tile, gm_width, da_width, q_scale):
    t, d = x2.shape
    tm = min(TM_PROJ, t)
    cols = w_in_b.shape[1]
    const = lambda shape: pl.BlockSpec(shape, lambda i: (0,) * len(shape))
    row_blk = lambda w: pl.BlockSpec((tm, w), lambda i: (i, 0))
    return pl.pallas_call(
        functools.partial(_proj_gmlp_kernel, gm_width=gm_width, da_width=da_width, q_scale=q_scale),
        grid=(t // tm,),
        in_specs=[row_blk(d), const((d, cols)), const((1, gm_width)), const((1, gm_width)),
                  const((GM_HEADS, CHUNK, CHUNK)), const((CHUNK, gm_width))],
        out_specs=[row_blk(gm_width), row_blk(da_width), row_blk(da_width), row_blk(da_width)],
        out_shape=[jax.ShapeDtypeStruct((t, gm_width), BF16)] + [jax.ShapeDtypeStruct((t, da_width), BF16)] * 3,
        compiler_params=pltpu.CompilerParams(dimension_semantics=("arbitrary",),
                                             vmem_limit_bytes=VMEM_LIMIT),
        name="proj_gmlp",
    )(x2, w_in_b, gm_g, gm_b, w_s, bs_tile)


def _diff_attn_kernel(slope_ref, q_ref, k_ref, v_ref, lq1_ref, lk1_ref, lq2_ref, lk2_ref, g_ref, o_ref,
                      *, tq, tk):
    h = pl.program_id(1)
    qi = pl.program_id(2)
    slope2 = slope_ref[h]
    q = q_ref[...]
    dv = q.shape[1]
    dk = dv // 2
    lane = lax.broadcasted_iota(jnp.int32, q.shape, 1)
    zero = jnp.zeros_like(q)
    q_maps = (jnp.where(lane < dk, q, zero), jnp.where(lane >= dk, q, zero))
    rowi = lax.broadcasted_iota(jnp.int32, (tq, tk), 0)
    coli = lax.broadcasted_iota(jnp.int32, (tq, tk), 1)
    dloc = coli - rowi
    bias_loc = dloc.astype(F32) * slope2

    def block(kj, carry, masked):
        kstart = pl.multiple_of(kj * tk, tk)
        kb = k_ref[pl.ds(kstart, tk), :]
        vb = v_ref[pl.ds(kstart, tk), :]
        rel = kstart - qi * tq
        off = rel.astype(F32) * slope2
        out = []
        for mp in range(2):
            m, l, acc = carry[3 * mp:3 * mp + 3]
            t = _nt_dot(q_maps[mp], kb) + bias_loc
            if masked:
                t = jnp.where(dloc + rel <= 0, t, NEG_BIG)
            m_new = jnp.maximum(m, jnp.max(t, axis=1, keepdims=True) + off)
            p = jnp.exp2(t + (off - m_new))
            alpha = jnp.exp2(m - m_new)
            l = alpha * l + jnp.sum(p, axis=1, keepdims=True)
            acc = alpha * acc + jnp.dot(p.astype(BF16), vb, preferred_element_type=F32)
            out += [m_new, l, acc]
        return tuple(out)

    init = (jnp.full((tq, 1), NEG_BIG, F32), jnp.zeros((tq, 1), F32), jnp.zeros((tq, dv), F32)) * 2
    n_full = (qi * tq) // tk
    carry = lax.fori_loop(0, n_full, lambda kj, c: block(kj, c, False), init)
    m1, l1, a1, m2, l2, a2 = block(n_full, carry, True)

    lam = (jnp.exp(jnp.sum(lq1_ref[...] * lk1_ref[...], axis=1, keepdims=True))
           - jnp.exp(jnp.sum(lq2_ref[...] * lk2_ref[...], axis=1, keepdims=True)) + LAMBDA_INIT)
    att = a1 / l1 - lam * (a2 / l2)
    ms = jnp.mean(att * att, axis=1, keepdims=True)
    y = att * lax.rsqrt(ms + LN_EPS) * g_ref[...]
    o_ref[...] = (y * (1.0 - LAMBDA_INIT)).astype(o_ref.dtype)


def _diff_attn(q3, k3, v3, slopes2, lq1, lk1, lq2, lk2, g3):
    b, s, w = q3.shape
    dv = w // DA_HEADS
    tq = min(TQ, s)
    tk = min(TK, s)
    dk = lq1.shape[1]
    vec = pl.BlockSpec((1, dk), lambda bi, hi, qi: (0, 0))
    return pl.pallas_call(
        functools.partial(_diff_attn_kernel, tq=tq, tk=tk),
        grid=(b, DA_HEADS, s // tq),
        in_specs=[pl.BlockSpec(memory_space=pltpu.SMEM),
                  pl.BlockSpec((None, tq, dv), lambda bi, hi, qi: (bi, qi, hi)),
                  pl.BlockSpec((None, s, dv), lambda bi, hi, qi: (bi, 0, hi)),
                  pl.BlockSpec((None, s, dv), lambda bi, hi, qi: (bi, 0, hi)),
                  vec, vec, vec, vec,
                  pl.BlockSpec((None, 1, dv), lambda bi, hi, qi: (hi, 0, 0))],
        out_specs=pl.BlockSpec((None, tq, dv), lambda bi, hi, qi: (bi, qi, hi)),
        out_shape=jax.ShapeDtypeStruct((b, s, w), BF16),
        compiler_params=pltpu.CompilerParams(dimension_semantics=("arbitrary",) * 3,
                                             vmem_limit_bytes=VMEM_LIMIT),
        name="diff_attn",
    )(slopes2, q3, k3, v3, lq1, lk1, lq2, lk2, g3)


def _mix_ln_peerq_kernel(ygm_ref, yda_ref, x_ref, wo_ref, g_ref, b_ref, wq_ref, ka_ref, kb_ref,
                         x1_ref, x1t_ref, sa_ref, sb_ref):
    gw = ygm_ref.shape[1]
    mix = (jnp.dot(ygm_ref[...], wo_ref[:gw, :], preferred_element_type=F32)
           + jnp.dot(yda_ref[...], wo_ref[gw:, :], preferred_element_type=F32))
    x1 = _layer_norm_rows(ALPHA * x_ref[...] + mix, g_ref[...], b_ref[...])
    x1_ref[...] = x1
    x1b = x1.astype(BF16)
    x1t_ref[...] = x1.T.astype(BF16)
    half = ka_ref.shape[2]
    for hh in range(PEER_HEADS):
        c0 = hh * 2 * half
        qa = jnp.dot(x1b, wq_ref[:, c0:c0 + half], preferred_element_type=F32).astype(BF16)
        qb = jnp.dot(x1b, wq_ref[:, c0 + half:c0 + 2 * half], preferred_element_type=F32).astype(BF16)
        sa_ref[hh] = _nt_dot(ka_ref[hh], qa)
        sb_ref[hh] = _nt_dot(kb_ref[hh], qb)


def _mix_ln_peerq(ygm, yda, x2, wo_b, ln_g, ln_b, wq_b, ka_b, kb_b):
    t, d = x2.shape
    tm = min(TM_PROJ, t)
    gw, dw = ygm.shape[1], yda.shape[1]
    const = lambda shape: pl.BlockSpec(shape, lambda i: (0,) * len(shape))
    row_blk = lambda w: pl.BlockSpec((tm, w), lambda i: (i, 0))
    score_blk = pl.BlockSpec((PEER_HEADS, N_KEYS, tm), lambda i: (0, 0, i))
    score_shape = jax.ShapeDtypeStruct((PEER_HEADS, N_KEYS, t), F32)
    return pl.pallas_call(
        _mix_ln_peerq_kernel,
        grid=(t // tm,),
        in_specs=[row_blk(gw), row_blk(dw), row_blk(d), const(wo_b.shape), const((1, d)), const((1, d)),
                  const(wq_b.shape), const(ka_b.shape), const(kb_b.shape)],
        out_specs=[row_blk(d), pl.BlockSpec((d, tm), lambda i: (0, i)), score_blk, score_blk],
        out_shape=[jax.ShapeDtypeStruct((t, d), F32), jax.ShapeDtypeStruct((d, t), BF16),
                   score_shape, score_shape],
        compiler_params=pltpu.CompilerParams(dimension_semantics=("arbitrary",),
                                             vmem_limit_bytes=VMEM_LIMIT),
        name="mix_ln_peerq",
    )(ygm, yda, x2, wo_b, ln_g, ln_b, wq_b, ka_b, kb_b)


def _top16(val, key_iota, vals_ref):
    rank = jnp.full(val.shape, NOT_RANKED, F32)
    for r in range(PEER_TOPK):
        m = jnp.max(val, axis=0, keepdims=True)
        idx = jnp.min(jnp.where(val == m, key_iota, float(N_KEYS)), axis=0, keepdims=True)
        hit = key_iota == idx
        rank = jnp.where(hit, float(r), rank)
        val = jnp.where(hit, -jnp.inf, val)
        vals_ref[r:r + 1, :] = m
    return rank


_PIECES = 10
_J_LIMIT = (8, 8, 8, 5, 4, 3, 2, 2, 2, 8)


def _peer_select_kernel(sa_ref, sb_ref, cnt_ref, wa_ref, rb_ref, wb_ref, va_ref, vb_ref):
    sa = sa_ref[...]
    sb = sb_ref[...]
    tl = sa.shape[1]
    key_iota = lax.broadcasted_iota(jnp.int32, sa.shape, 0).astype(F32)
    ra = _top16(sa, key_iota, va_ref)
    rb = _top16(sb, key_iota, vb_ref)
    va = va_ref[...]
    vb = vb_ref[...]
    j8 = lax.broadcasted_iota(jnp.int32, (8, tl), 0).astype(F32)
    pieces, flats = [], []
    for p in range(_PIECES):
        if p == 0:
            c, f = vb[0:8] + va[0:1], j8
        elif p == 1:
            c, f = vb[8:16] + va[0:1], j8 + 8.0
        elif p == 9:
            c, f = va[8:16] + vb[0:1], 128.0 + 16.0 * j8
        else:
            i = p - 1
            c, f = vb[0:8] + va[i:i + 1], j8 + 16.0 * i
            c = jnp.where(j8 < float(_J_LIMIT[p]), c, -jnp.inf)
        pieces.append(c)
        flats.append(f)
    cand = jnp.concatenate(pieces, axis=0)
    flat = jnp.concatenate(flats, axis=0)
    i16 = lax.broadcasted_iota(jnp.int32, (PEER_TOPK, tl), 0).astype(F32)
    cnt = jnp.zeros((PEER_TOPK, tl), F32)
    s0 = va[0:1] + vb[0:1]
    z = jnp.zeros((1, tl), F32)
    for _ in range(PEER_TOPK):
        m = jnp.max(cand, axis=0, keepdims=True)
        idx = jnp.min(jnp.where(cand == m, flat, 1024.0), axis=0, keepdims=True)
        cand = jnp.where(flat == idx, -jnp.inf, cand)
        cnt = cnt + jnp.where(i16 == jnp.floor(idx * (1.0 / PEER_TOPK)), 1.0, 0.0)
        z = z + jnp.exp(m - s0)
    cnt_key = jnp.zeros(sa.shape, F32)
    for i in range(PEER_TOPK):
        cnt_key = jnp.where(ra == float(i), cnt[i:i + 1], cnt_key)
    cnt_ref[...] = cnt_key
    wa_ref[...] = jnp.exp(sa - va[0:1]) * (0.5 / z)
    rb_ref[...] = rb.astype(BF16)
    wb_ref[...] = jnp.exp(sb - vb[0:1]).astype(BF16)


def _peer_select(sa, sb):
    nh, nk, t = sa.shape
    tl = min(TL_SEL, t)
    blk = pl.BlockSpec((None, nk, tl), lambda i, h: (h, 0, i))
    return pl.pallas_call(
        _peer_select_kernel,
        grid=(t // tl, nh),
        in_specs=[blk, blk],
        out_specs=[blk, blk, blk, blk],
        out_shape=[jax.ShapeDtypeStruct(sa.shape, F32), jax.ShapeDtypeStruct(sa.shape, F32),
                   jax.ShapeDtypeStruct(sa.shape, BF16), jax.ShapeDtypeStruct(sa.shape, BF16)],
        scratch_shapes=[pltpu.VMEM((PEER_TOPK, tl), F32), pltpu.VMEM((PEER_TOPK, tl), F32)],
        compiler_params=pltpu.CompilerParams(dimension_semantics=("arbitrary", "arbitrary"),
                                             vmem_limit_bytes=VMEM_LIMIT),
        name="peer_select",
    )(sa, sb)


def _peer_dense_kernel(x1t_ref, u_ref, vt_ref, cnt_ref, wa_ref, rb_ref, wb_ref, o_ref, s_ref, *, sub):
    j = pl.program_id(1)
    eb = u_ref.shape[0]
    tt = x1t_ref.shape[1]
    n_sub = tt // sub
    a_rows = eb // N_KEYS
    a_grp = 2
    n_rg = N_KEYS // SUBLANES
    rg_grp = 8
    zero = jnp.zeros((), BF16)

    @pl.when(j == 0)
    def _():
        o_ref[...] = jnp.zeros_like(o_ref)

    def pre_act(c):
        s_ref[c % 2] = jnp.dot(u_ref[...], x1t_ref[:, c * sub:(c + 1) * sub], preferred_element_type=F32)

    def gate_and_activate(c):
        ts = slice(c * sub, (c + 1) * sub)
        cnt_b = [cnt_ref[hh, :, ts].astype(BF16) for hh in range(PEER_HEADS)]
        wa_b = [wa_ref[hh, :, ts].astype(BF16) for hh in range(PEER_HEADS)]
        pieces = [None] * (eb // SUBLANES)
        for a0 in range(0, a_rows, a_grp):
            for r0 in range(0, n_rg, rg_grp):
                g = [[jnp.zeros((SUBLANES, sub), BF16) for _ in range(rg_grp)] for _ in range(a_grp)]
                for hh in range(PEER_HEADS):
                    cnt = [jnp.broadcast_to(cnt_b[hh][a0 + k:a0 + k + 1], (SUBLANES, sub)) for k in range(a_grp)]
                    wa = [jnp.broadcast_to(wa_b[hh][a0 + k:a0 + k + 1], (SUBLANES, sub)) for k in range(a_grp)]
                    for r in range(rg_grp):
                        bs = slice((r0 + r) * SUBLANES, (r0 + r + 1) * SUBLANES)
                        rb = rb_ref[hh, bs, ts]
                        wb = wb_ref[hh, bs, ts]
                        for k in range(a_grp):
                            g[k][r] = g[k][r] + jnp.where(rb < cnt[k], wb * wa[k], zero)
                for k in range(a_grp):
                    for r in range(rg_grp):
                        e0 = (a0 + k) * N_KEYS + (r0 + r) * SUBLANES
                        x = s_ref[c % 2, e0:e0 + SUBLANES, :]
                        act = x * (1.0 + lax.erf(x * (1.0 / math.sqrt(2.0))))
                        pieces[e0 // SUBLANES] = act.astype(BF16) * g[k][r]
        return jnp.concatenate(pieces, axis=0)

    pre_act(0)
    for c in range(n_sub):
        if c + 1 < n_sub:
            pre_act(c + 1)
        hact = gate_and_activate(c)
        o_ref[:, c * sub:(c + 1) * sub] += jnp.dot(vt_ref[...], hact, preferred_element_type=F32)


def _peer_dense(x1t, u_b, vt_b, cnt, wa, rb, wb):
    d, t = x1t.shape
    ne = u_b.shape[0]
    tt = min(TT_DENSE, t)
    sub = min(SUB_DENSE, tt)
    eb = EB_DENSE
    a_rows = eb // N_KEYS
    row_meta = pl.BlockSpec((PEER_HEADS, a_rows, tt), lambda i, j: (0, j, i))
    key_meta = pl.BlockSpec((PEER_HEADS, N_KEYS, tt), lambda i, j: (0, 0, i))
    return pl.pallas_call(
        functools.partial(_peer_dense_kernel, sub=sub),
        grid=(t // tt, ne // eb),
        in_specs=[pl.BlockSpec((d, tt), lambda i, j: (0, i)),
                  pl.BlockSpec((eb, d), lambda i, j: (j, 0)),
                  pl.BlockSpec((d, eb), lambda i, j: (0, j)),
                  row_meta, row_meta, key_meta, key_meta],
        out_specs=pl.BlockSpec((d, tt), lambda i, j: (0, i)),
        out_shape=jax.ShapeDtypeStruct((d, t), F32),
        scratch_shapes=[pltpu.VMEM((2, eb, sub), F32)],
        compiler_params=pltpu.CompilerParams(dimension_semantics=("arbitrary", "arbitrary"),
                                             vmem_limit_bytes=VMEM_LIMIT),
        name="peer_dense",
    )(x1t, u_b, vt_b, cnt, wa, rb, wb)


def _ffn_ln_kernel(ffnt_ref, x1_ref, g_ref, b_ref, o_ref):
    ffn = ffnt_ref[...].T
    o_ref[...] = _layer_norm_rows(ALPHA * x1_ref[...] + ffn, g_ref[...], b_ref[...])


def _ffn_ln(ffnt, x1, ln_g, ln_b):
    t, d = x1.shape
    tm = min(TM_PROJ, t)
    const = lambda shape: pl.BlockSpec(shape, lambda i: (0,) * len(shape))
    return pl.pallas_call(
        _ffn_ln_kernel,
        grid=(t // tm,),
        in_specs=[pl.BlockSpec((d, tm), lambda i: (0, i)), pl.BlockSpec((tm, d), lambda i: (i, 0)),
                  const((1, d)), const((1, d))],
        out_specs=pl.BlockSpec((tm, d), lambda i: (i, 0)),
        out_shape=jax.ShapeDtypeStruct((t, d), F32),
        compiler_params=pltpu.CompilerParams(dimension_semantics=("arbitrary",),
                                             vmem_limit_bytes=VMEM_LIMIT),
        name="ffn_ln",
    )(ffnt, x1, ln_g, ln_b)


def kernel(x, w_in, gm_norm_g, gm_norm_b, gm_w_s, gm_b_s, lam_q1, lam_k1, lam_q2, lam_k2, da_norm_g,
           w_o, ln1_g, ln1_b, peer_w_q, peer_keys_a, peer_keys_b, peer_u, peer_v, ln2_g, ln2_b):
    b, s, d = x.shape
    t = b * s
    gm_width = gm_norm_g.shape[0]
    da_width = da_norm_g.shape[0] * da_norm_g.shape[1]
    dk = lam_q1.shape[0]
    assert s % CHUNK == 0 and gm_w_s.shape == (GM_HEADS, CHUNK, CHUNK)
    assert peer_u.shape[0] == N_KEYS * N_KEYS and peer_keys_a.shape[:2] == (PEER_HEADS, N_KEYS)

    row = lambda v: v.reshape(1, -1).astype(F32)
    x2 = x.reshape(t, d)
    bs_tile = jnp.repeat(gm_b_s.T, gm_width // GM_HEADS, axis=1)
    q_scale = (dk ** -0.5) * LOG2E
    ygm, q, k, v = _proj_gmlp(x2, w_in.astype(BF16), row(gm_norm_g), row(gm_norm_b), gm_w_s, bs_tile,
                              gm_width, da_width, q_scale)

    slopes2 = jnp.asarray(2.0 ** (-(8.0 / DA_HEADS) * np.arange(1, DA_HEADS + 1)) * LOG2E, F32)
    yda = _diff_attn(q.reshape(b, s, da_width), k.reshape(b, s, da_width), v.reshape(b, s, da_width),
                     slopes2, row(lam_q1), row(lam_k1), row(lam_q2), row(lam_k2),
                     da_norm_g.reshape(DA_HEADS, 1, -1).astype(F32))

    x1, x1t, sa, sb = _mix_ln_peerq(ygm, yda.reshape(t, da_width), x2, w_o.astype(BF16), row(ln1_g),
                                    row(ln1_b), peer_w_q.astype(BF16), peer_keys_a.astype(BF16),
                                    peer_keys_b.astype(BF16))
    cnt, wa, rb, wb = _peer_select(sa, sb)
    ffnt = _peer_dense(x1t, peer_u.astype(BF16), peer_v.astype(BF16).T, cnt, wa, rb, wb)
    out = _ffn_ln(ffnt, x1, row(ln2_g), row(ln2_b))
    return out.reshape(b, s, d)
```

```python
import functools
import math

import jax
import jax.numpy as jnp
import numpy as np
from jax import lax
from jax.experimental import pallas as pl
from jax.experimental.pallas import tpu as pltpu

F32 = jnp.float32
BF16 = jnp.bfloat16

GM_HEADS = 4
CHUNK = 128
DA_HEADS = 4
LAMBDA_INIT = 0.8 - 0.6 * math.exp(-0.3 * (1 - 1))
PEER_HEADS = 8
N_KEYS = 128
PEER_TOPK = 16
DEPTH = 1
ALPHA = (2.0 * DEPTH) ** 0.25
LN_EPS = 1e-5
NEG_BIG = -1e30
LOG2E = 1.4426950408889634
NOT_RANKED = 64.0

LANES = 128
SUBLANES = 8
VMEM_LIMIT = 48 * 1024 * 1024
TM_PROJ = 512
TQ = 512
SQ = 256
TK = 512
RC_ATTN = 64
TL_SEL = 512
TT_DENSE = 1024
SUB_DENSE = 256
EB_DENSE = 1024


def _gelu_exact(x):
    return 0.5 * x * (1.0 + lax.erf(x * (1.0 / math.sqrt(2.0))))


def _layer_norm_rows(x, g, b):
    mu = jnp.mean(x, axis=-1, keepdims=True)
    xc = x - mu
    var = jnp.mean(xc * xc, axis=-1, keepdims=True)
    return xc * lax.rsqrt(var + LN_EPS) * g + b


def _nt_dot(a, b):
    return lax.dot_general(a, b, (((1,), (1,)), ((), ())), preferred_element_type=F32)


def _proj_gmlp_kernel(x_ref, w_ref, wqt_ref, wvt_ref, g_ref, b_ref, ws_ref, bs_ref,
                      ygm_ref, qt_ref, k_ref, vt_ref, *, gm_width, da_width, q_scale):
    xb = x_ref[...].astype(BF16)
    tm = xb.shape[0]
    hd = gm_width // GM_HEADS
    u = _gelu_exact(jnp.dot(xb, w_ref[:, :gm_width], preferred_element_type=F32))
    z = _gelu_exact(jnp.dot(xb, w_ref[:, gm_width:2 * gm_width], preferred_element_type=F32))
    vn = _layer_norm_rows(z, g_ref[...], b_ref[...]).astype(BF16)
    row = lax.broadcasted_iota(jnp.int32, (CHUNK, CHUNK), 0)
    col = lax.broadcasted_iota(jnp.int32, (CHUNK, CHUNK), 1)
    causal = col <= row
    for hh in range(GM_HEADS):
        w_c = jnp.where(causal, ws_ref[hh], 0.0).astype(BF16)
        for c in range(tm // CHUNK):
            rs = slice(c * CHUNK, (c + 1) * CHUNK)
            cs = slice(hh * hd, (hh + 1) * hd)
            mixed = jnp.dot(w_c, vn[rs, cs], preferred_element_type=F32) + bs_ref[:, cs]
            ygm_ref[rs, cs] = (u[rs, cs] * mixed).astype(BF16)
    o = 2 * gm_width
    k_ref[...] = jnp.dot(xb, w_ref[:, o + da_width:o + 2 * da_width],
                         preferred_element_type=F32).astype(BF16)
    qt_ref[...] = (_nt_dot(wqt_ref[...], xb) * q_scale).astype(BF16)
    vt_ref[...] = _nt_dot(wvt_ref[...], xb).astype(BF16)


def _proj_gmlp(x2, w_in_b, gm_g, gm_b, w_s, bs_tile, gm_width, da_width, q_scale):
    t, d = x2.shape
    tm = TK
    cols = w_in_b.shape[1]
    o = 2 * gm_width
    wqt = w_in_b[:, o:o + da_width].T
    wvt = w_in_b[:, o + 2 * da_width:o + 3 * da_width].T
    const = lambda shape: pl.BlockSpec(shape, lambda i: (0,) * len(shape))
    row_blk = lambda w: pl.BlockSpec((tm, w), lambda i: (i, 0))
    return pl.pallas_call(
        functools.partial(_proj_gmlp_kernel, gm_width=gm_width, da_width=da_width, q_scale=q_scale),
        grid=(t // tm,),
        in_specs=[row_blk(d), const((d, cols)), const((da_width, d)), const((da_width, d)),
                  const((1, gm_width)), const((1, gm_width)),
                  const((GM_HEADS, CHUNK, CHUNK)), const((CHUNK, gm_width))],
        out_specs=[row_blk(gm_width), pl.BlockSpec((da_width, tm), lambda i: (0, i)), row_blk(da_width),
                   pl.BlockSpec((None, da_width, tm), lambda i: (i, 0, 0))],
        out_shape=[jax.ShapeDtypeStruct((t, gm_width), BF16), jax.ShapeDtypeStruct((da_width, t), BF16),
                   jax.ShapeDtypeStruct((t, da_width), BF16),
                   jax.ShapeDtypeStruct((t // tm, da_width, tm), BF16)],
        compiler_params=pltpu.CompilerParams(dimension_semantics=("arbitrary",),
                                             vmem_limit_bytes=VMEM_LIMIT),
        name="proj_gmlp",
    )(x2, w_in_b, wqt, wvt, gm_g, gm_b, w_s, bs_tile)


def _diff_attn_kernel(slope_ref, qt_ref, k_ref, vt_ref, lq1_ref, lk1_ref, lq2_ref, lk2_ref, gb_ref, o_ref,
                      s_scr, p_scr, bias_scr, acc_scr, *, tq, tk, sq, rc):
    h = pl.program_id(1)
    qi = pl.program_id(2)
    slope2 = slope_ref[h]
    dv = qt_ref.shape[0]
    dk = dv // 2
    n_sq = tq // sq
    n_ch = 2 * n_sq
    n_full = qi

    @pl.when(qi == 0)
    def _():
        kr = lax.broadcasted_iota(jnp.int32, (tk, sq), 0)
        qc = lax.broadcasted_iota(jnp.int32, (tk, sq), 1)
        bias = (kr - qc).astype(F32) * slope2
        bias_scr[0] = bias
        for si in range(n_sq):
            bias_scr[1 + si] = jnp.where(kr <= qc + si * sq, bias, NEG_BIG)

    rowq = lax.broadcasted_iota(jnp.int32, (dv, sq), 0)
    chains = []
    for si in range(n_sq):
        qt = qt_ref[:, si * sq:(si + 1) * sq]
        zero = jnp.zeros_like(qt)
        chains.append((si, jnp.where(rowq < dk, qt, zero)))
        chains.append((si, jnp.where(rowq >= dk, qt, zero)))

    def logits_to_scratch(kj, slot):
        kb = k_ref[pl.ds(pl.multiple_of(kj * tk, tk), tk), :]
        for ci, (_, qm) in enumerate(chains):
            s_scr[slot, ci] = jnp.dot(kb, qm, preferred_element_type=F32)

    def softmax_step(kj, slot, ml, diag):
        out, alphas = [], []
        for ci, (si, _) in enumerate(chains):
            m, l = ml[2 * ci], ml[2 * ci + 1]
            rel = kj * tk - (qi * tq + si * sq)
            off = rel.astype(F32) * slope2
            table = 1 + si if diag else 0

            def logits(r0):
                return s_scr[slot, ci, r0:r0 + rc, :] + bias_scr[table, r0:r0 + rc, :]

            cmax = logits(0)
            for r0 in range(rc, tk, rc):
                cmax = jnp.maximum(cmax, logits(r0))
            m_new = jnp.maximum(m, jnp.max(cmax, axis=0, keepdims=True) + off)
            shift = off - m_new
            csum = jnp.zeros((rc, sq), F32)
            for r0 in range(0, tk, rc):
                p = jnp.exp2(logits(r0) + shift)
                csum = csum + p
                p_scr[slot, ci, r0:r0 + rc, :] = p.astype(BF16)
            alpha = jnp.exp2(m - m_new)
            out += [m_new, alpha * l + jnp.sum(csum, axis=0, keepdims=True)]
            alphas.append(alpha)
        return tuple(out), alphas

    def accumulate_previous(kj_prev, slot_prev, alphas):
        vb = vt_ref[kj_prev]
        for ci in range(n_ch):
            pv = jnp.dot(vb, p_scr[slot_prev, ci], preferred_element_type=F32)
            acc_scr[ci] = alphas[ci] * (acc_scr[ci] + pv)

    def step(kj, slot, ml, prefetch_logits, diag):
        if prefetch_logits:
            logits_to_scratch(kj + 1, 1 - slot)
        ml, alphas = softmax_step(kj, slot, ml, diag)
        accumulate_previous(jnp.maximum(kj - 1, 0), 1 - slot, alphas)
        return ml

    def finalize(kj, slot, ml):
        vb = vt_ref[kj]
        lam = (jnp.exp(jnp.sum(lq1_ref[...] * lk1_ref[...], axis=1, keepdims=True))
               - jnp.exp(jnp.sum(lq2_ref[...] * lk2_ref[...], axis=1, keepdims=True)) + LAMBDA_INIT)
        g = jnp.concatenate([gb_ref[...]] * (sq // gb_ref.shape[1]), axis=1)
        for si in range(n_sq):
            a = [acc_scr[2 * si + mp] + jnp.dot(vb, p_scr[slot, 2 * si + mp], preferred_element_type=F32)
                 for mp in range(2)]
            l1, l2 = ml[4 * si + 1], ml[4 * si + 3]
            att = a[0] / l1 - lam * (a[1] / l2)
            ms = jnp.mean(att * att, axis=0, keepdims=True)
            y = att * lax.rsqrt(ms + LN_EPS) * g
            o_ref[si * sq:(si + 1) * sq, :] = (y * (1.0 - LAMBDA_INIT)).T.astype(o_ref.dtype)

    acc_scr[...] = jnp.zeros_like(acc_scr)
    p_scr[1] = jnp.zeros(p_scr.shape[1:], BF16)
    logits_to_scratch(0, 0)
    ml0 = (jnp.full((1, sq), NEG_BIG, F32), jnp.zeros((1, sq), F32)) * n_ch

    def pair(i, ml):
        ml = step(2 * i, 0, ml, True, False)
        return step(2 * i + 1, 1, ml, True, False)

    ml = lax.fori_loop(0, n_full // 2, pair, ml0)

    @pl.when(n_full % 2 == 0)
    def _():
        finalize(n_full, 0, step(n_full, 0, ml, False, True))

    @pl.when(n_full % 2 == 1)
    def _():
        ml1 = step(n_full - 1, 0, ml, True, False)
        finalize(n_full, 1, step(n_full, 1, ml1, False, True))


def _diff_attn(qt, k3, vt3, slopes2, lq1, lk1, lq2, lk2, gb):
    b, s, w = k3.shape
    dv = w // DA_HEADS
    tq, tk, sq = min(TQ, s), min(TK, s), min(SQ, s)
    assert tq == tk
    nq, nkb = s // tq, s // tk
    n_ch = 2 * (tq // sq)
    dk = lq1.shape[1]
    vec = pl.BlockSpec((1, dk), lambda bi, hi, qi: (0, 0))
    return pl.pallas_call(
        functools.partial(_diff_attn_kernel, tq=tq, tk=tk, sq=sq, rc=RC_ATTN),
        scratch_shapes=[pltpu.VMEM((2, n_ch, tk, sq), F32), pltpu.VMEM((2, n_ch, tk, sq), BF16),
                        pltpu.VMEM((1 + tq // sq, tk, sq), F32), pltpu.VMEM((n_ch, dv, sq), F32)],
        grid=(b, DA_HEADS, nq),
        in_specs=[pl.BlockSpec(memory_space=pltpu.SMEM),
                  pl.BlockSpec((dv, tq), lambda bi, hi, qi: (hi, bi * nq + qi)),
                  pl.BlockSpec((None, s, dv), lambda bi, hi, qi: (bi, 0, hi)),
                  pl.BlockSpec((nkb, dv, tk), lambda bi, hi, qi: (bi, hi, 0)),
                  vec, vec, vec, vec,
                  pl.BlockSpec((None, dv, LANES), lambda bi, hi, qi: (hi, 0, 0))],
        out_specs=pl.BlockSpec((None, tq, dv), lambda bi, hi, qi: (bi, qi, hi)),
        out_shape=jax.ShapeDtypeStruct((b, s, w), BF16),
        compiler_params=pltpu.CompilerParams(dimension_semantics=("arbitrary",) * 3,
                                             vmem_limit_bytes=VMEM_LIMIT),
        name="diff_attn",
    )(slopes2, qt, k3, vt3, lq1, lk1, lq2, lk2, gb)


def _mix_ln_peerq_kernel(ygm_ref, yda_ref, x_ref, wo_ref, g_ref, b_ref, wq_ref, ka_ref, kb_ref,
                         x1_ref, x1t_ref, sa_ref, sb_ref):
    gw = ygm_ref.shape[1]
    mix = (jnp.dot(ygm_ref[...], wo_ref[:gw, :], preferred_element_type=F32)
           + jnp.dot(yda_ref[...], wo_ref[gw:, :], preferred_element_type=F32))
    x1 = _layer_norm_rows(ALPHA * x_ref[...] + mix, g_ref[...], b_ref[...])
    x1_ref[...] = x1
    x1b = x1.astype(BF16)
    x1t_ref[...] = x1.T.astype(BF16)
    half = ka_ref.shape[2]
    for hh in range(PEER_HEADS):
        c0 = hh * 2 * half
        qa = jnp.dot(x1b, wq_ref[:, c0:c0 + half], preferred_element_type=F32).astype(BF16)
        qb = jnp.dot(x1b, wq_ref[:, c0 + half:c0 + 2 * half], preferred_element_type=F32).astype(BF16)
        sa_ref[hh] = _nt_dot(ka_ref[hh], qa)
        sb_ref[hh] = _nt_dot(kb_ref[hh], qb)


def _mix_ln_peerq(ygm, yda, x2, wo_b, ln_g, ln_b, wq_b, ka_b, kb_b):
    t, d = x2.shape
    tm = min(TM_PROJ, t)
    gw, dw = ygm.shape[1], yda.shape[1]
    const = lambda shape: pl.BlockSpec(shape, lambda i: (0,) * len(shape))
    row_blk = lambda w: pl.BlockSpec((tm, w), lambda i: (i, 0))
    score_blk = pl.BlockSpec((PEER_HEADS, N_KEYS, tm), lambda i: (0, 0, i))
    score_shape = jax.ShapeDtypeStruct((PEER_HEADS, N_KEYS, t), F32)
    return pl.pallas_call(
        _mix_ln_peerq_kernel,
        grid=(t // tm,),
        in_specs=[row_blk(gw), row_blk(dw), row_blk(d), const(wo_b.shape), const((1, d)), const((1, d)),
                  const(wq_b.shape), const(ka_b.shape), const(kb_b.shape)],
        out_specs=[row_blk(d), pl.BlockSpec((d, tm), lambda i: (0, i)), score_blk, score_blk],
        out_shape=[jax.ShapeDtypeStruct((t, d), F32), jax.ShapeDtypeStruct((d, t), BF16),
                   score_shape, score_shape],
        compiler_params=pltpu.CompilerParams(dimension_semantics=("arbitrary",),
                                             vmem_limit_bytes=VMEM_LIMIT),
        name="mix_ln_peerq",
    )(ygm, yda, x2, wo_b, ln_g, ln_b, wq_b, ka_b, kb_b)


def _top16_exact(val, rank_ref, vals_ref):
    key_iota = lax.broadcasted_iota(jnp.int32, val.shape, 0).astype(F32)
    rank = jnp.full(val.shape, NOT_RANKED, F32)
    for r in range(PEER_TOPK):
        m = jnp.max(val, axis=0, keepdims=True)
        idx = jnp.min(jnp.where(val == m, key_iota, float(N_KEYS)), axis=0, keepdims=True)
        hit = key_iota == idx
        rank = jnp.where(hit, float(r), rank)
        val = jnp.where(hit, -jnp.inf, val)
        vals_ref[r:r + 1, :] = m
    rank_ref[...] = rank


def _top16_if_distinct(val, rank_ref, vals_ref):
    rank = jnp.full(val.shape, NOT_RANKED, F32)
    for r in range(PEER_TOPK):
        m = jnp.max(val, axis=0, keepdims=True)
        hit = val == m
        rank = jnp.where(hit, float(r), rank)
        val = jnp.where(hit, -jnp.inf, val)
        vals_ref[r:r + 1, :] = m
    rank_ref[...] = rank
    return jnp.sum(jnp.where(val == -jnp.inf, 1.0, 0.0), axis=0, keepdims=True)


_PIECES = 10
_J_LIMIT = (8, 8, 8, 5, 4, 3, 2, 2, 2, 8)


def _peer_select_kernel(sa_ref, sb_ref, cnt_ref, wa_ref, rb_ref, wb_ref, va_ref, vb_ref, ra_scr, rb_scr):
    sa = sa_ref[...]
    sb = sb_ref[...]
    tl = sa.shape[1]
    extracted = jnp.maximum(_top16_if_distinct(sa, ra_scr, va_ref), _top16_if_distinct(sb, rb_scr, vb_ref))

    @pl.when(jnp.max(extracted) > float(PEER_TOPK))
    def _():
        _top16_exact(sa, ra_scr, va_ref)
        _top16_exact(sb, rb_scr, vb_ref)

    ra = ra_scr[...]
    rb = rb_scr[...]
    va = va_ref[...]
    vb = vb_ref[...]
    j8 = lax.broadcasted_iota(jnp.int32, (8, tl), 0).astype(F32)
    pieces, flats = [], []
    for p in range(_PIECES):
        if p == 0:
            c, f = vb[0:8] + va[0:1], j8
        elif p == 1:
            c, f = vb[8:16] + va[0:1], j8 + 8.0
        elif p == 9:
            c, f = va[8:16] + vb[0:1], 128.0 + 16.0 * j8
        else:
            i = p - 1
            c, f = vb[0:8] + va[i:i + 1], j8 + 16.0 * i
            c = jnp.where(j8 < float(_J_LIMIT[p]), c, -jnp.inf)
        pieces.append(c)
        flats.append(f)
    cand = jnp.concatenate(pieces, axis=0)
    flat = jnp.concatenate(flats, axis=0)
    i16 = lax.broadcasted_iota(jnp.int32, (PEER_TOPK, tl), 0).astype(F32)
    cnt = jnp.zeros((PEER_TOPK, tl), F32)
    s0 = va[0:1] + vb[0:1]
    z = jnp.zeros((1, tl), F32)
    for _ in range(PEER_TOPK):
        m = jnp.max(cand, axis=0, keepdims=True)
        idx = jnp.min(jnp.where(cand == m, flat, 1024.0), axis=0, keepdims=True)
        cand = jnp.where(flat == idx, -jnp.inf, cand)
        cnt = cnt + jnp.where(i16 == jnp.floor(idx * (1.0 / PEER_TOPK)), 1.0, 0.0)
        z = z + jnp.exp(m - s0)
    cnt_key = jnp.zeros(sa.shape, F32)
    for i in range(PEER_TOPK):
        cnt_key = jnp.where(ra == float(i), cnt[i:i + 1], cnt_key)
    cnt_ref[...] = cnt_key
    wa_ref[...] = jnp.exp(sa - va[0:1]) * (0.5 / z)
    rb_ref[...] = rb.astype(BF16)
    wb_ref[...] = jnp.exp(sb - vb[0:1]).astype(BF16)


def _peer_select(sa, sb):
    nh, nk, t = sa.shape
    tl = min(TL_SEL, t)
    blk = pl.BlockSpec((None, nk, tl), lambda i, h: (h, 0, i))
    return pl.pallas_call(
        _peer_select_kernel,
        grid=(t // tl, nh),
        in_specs=[blk, blk],
        out_specs=[blk, blk, blk, blk],
        out_shape=[jax.ShapeDtypeStruct(sa.shape, F32), jax.ShapeDtypeStruct(sa.shape, F32),
                   jax.ShapeDtypeStruct(sa.shape, BF16), jax.ShapeDtypeStruct(sa.shape, BF16)],
        scratch_shapes=[pltpu.VMEM((PEER_TOPK, tl), F32), pltpu.VMEM((PEER_TOPK, tl), F32),
                        pltpu.VMEM((nk, tl), F32), pltpu.VMEM((nk, tl), F32)],
        compiler_params=pltpu.CompilerParams(dimension_semantics=("arbitrary", "arbitrary"),
                                             vmem_limit_bytes=VMEM_LIMIT),
        name="peer_select",
    )(sa, sb)


def _peer_dense_kernel(x1t_ref, u_ref, vt_ref, cnt_ref, wa_ref, rb_ref, wb_ref, o_ref, s_ref, *, sub):
    j = pl.program_id(1)
    eb = u_ref.shape[0]
    tt = x1t_ref.shape[1]
    n_sub = tt // sub
    a_rows = eb // N_KEYS
    a_grp = 2
    n_rg = N_KEYS // SUBLANES
    rg_grp = 8
    zero = jnp.zeros((), BF16)

    @pl.when(j == 0)
    def _():
        o_ref[...] = jnp.zeros_like(o_ref)

    def pre_act(c):
        s_ref[c % 2] = jnp.dot(u_ref[...], x1t_ref[:, c * sub:(c + 1) * sub], preferred_element_type=F32)

    def gate_and_activate(c):
        ts = slice(c * sub, (c + 1) * sub)
        cnt_b = [cnt_ref[hh, :, ts].astype(BF16) for hh in range(PEER_HEADS)]
        wa_b = [wa_ref[hh, :, ts].astype(BF16) for hh in range(PEER_HEADS)]
        pieces = [None] * (eb // SUBLANES)
        for a0 in range(0, a_rows, a_grp):
            for r0 in range(0, n_rg, rg_grp):
                g = [[jnp.zeros((SUBLANES, sub), BF16) for _ in range(rg_grp)] for _ in range(a_grp)]
                for hh in range(PEER_HEADS):
                    cnt = [jnp.broadcast_to(cnt_b[hh][a0 + k:a0 + k + 1], (SUBLANES, sub)) for k in range(a_grp)]
                    wa = [jnp.broadcast_to(wa_b[hh][a0 + k:a0 + k + 1], (SUBLANES, sub)) for k in range(a_grp)]
                    for r in range(rg_grp):
                        bs = slice((r0 + r) * SUBLANES, (r0 + r + 1) * SUBLANES)
                        rb = rb_ref[hh, bs, ts]
                        wb = wb_ref[hh, bs, ts]
                        for k in range(a_grp):
                            g[k][r] = g[k][r] + jnp.where(rb < cnt[k], wb * wa[k], zero)
                for k in range(a_grp):
                    for r in range(rg_grp):
                        e0 = (a0 + k) * N_KEYS + (r0 + r) * SUBLANES
                        x = s_ref[c % 2, e0:e0 + SUBLANES, :]
                        act = x * (1.0 + lax.erf(x * (1.0 / math.sqrt(2.0))))
                        pieces[e0 // SUBLANES] = act.astype(BF16) * g[k][r]
        return jnp.concatenate(pieces, axis=0)

    pre_act(0)
    for c in range(n_sub):
        if c + 1 < n_sub:
            pre_act(c + 1)
        hact = gate_and_activate(c)
        o_ref[:, c * sub:(c + 1) * sub] += jnp.dot(vt_ref[...], hact, preferred_element_type=F32)


def _peer_dense(x1t, u_b, vt_b, cnt, wa, rb, wb):
    d, t = x1t.shape
    ne = u_b.shape[0]
    tt = min(TT_DENSE, t)
    sub = min(SUB_DENSE, tt)
    eb = EB_DENSE
    a_rows = eb // N_KEYS
    row_meta = pl.BlockSpec((PEER_HEADS, a_rows, tt), lambda i, j: (0, j, i))
    key_meta = pl.BlockSpec((PEER_HEADS, N_KEYS, tt), lambda i, j: (0, 0, i))
    return pl.pallas_call(
        functools.partial(_peer_dense_kernel, sub=sub),
        grid=(t // tt, ne // eb),
        in_specs=[pl.BlockSpec((d, tt), lambda i, j: (0, i)),
                  pl.BlockSpec((eb, d), lambda i, j: (j, 0)),
                  pl.BlockSpec((d, eb), lambda i, j: (0, j)),
                  row_meta, row_meta, key_meta, key_meta],
        out_specs=pl.BlockSpec((d, tt), lambda i, j: (0, i)),
        out_shape=jax.ShapeDtypeStruct((d, t), F32),
        scratch_shapes=[pltpu.VMEM((2, eb, sub), F32)],
        compiler_params=pltpu.CompilerParams(dimension_semantics=("arbitrary", "arbitrary"),
                                             vmem_limit_bytes=VMEM_LIMIT),
        name="peer_dense",
    )(x1t, u_b, vt_b, cnt, wa, rb, wb)


def _ffn_ln_kernel(ffnt_ref, x1_ref, g_ref, b_ref, o_ref):
    ffn = ffnt_ref[...].T
    o_ref[...] = _layer_norm_rows(ALPHA * x1_ref[...] + ffn, g_ref[...], b_ref[...])


def _ffn_ln(ffnt, x1, ln_g, ln_b):
    t, d = x1.shape
    tm = min(TM_PROJ, t)
    const = lambda shape: pl.BlockSpec(shape, lambda i: (0,) * len(shape))
    return pl.pallas_call(
        _ffn_ln_kernel,
        grid=(t // tm,),
        in_specs=[pl.BlockSpec((d, tm), lambda i: (0, i)), pl.BlockSpec((tm, d), lambda i: (i, 0)),
                  const((1, d)), const((1, d))],
        out_specs=pl.BlockSpec((tm, d), lambda i: (i, 0)),
        out_shape=jax.ShapeDtypeStruct((t, d), F32),
        compiler_params=pltpu.CompilerParams(dimension_semantics=("arbitrary",),
                                             vmem_limit_bytes=VMEM_LIMIT),
        name="ffn_ln",
    )(ffnt, x1, ln_g, ln_b)


def kernel(x, w_in, gm_norm_g, gm_norm_b, gm_w_s, gm_b_s, lam_q1, lam_k1, lam_q2, lam_k2, da_norm_g,
           w_o, ln1_g, ln1_b, peer_w_q, peer_keys_a, peer_keys_b, peer_u, peer_v, ln2_g, ln2_b):
    b, s, d = x.shape
    t = b * s
    gm_width = gm_norm_g.shape[0]
    da_width = da_norm_g.shape[0] * da_norm_g.shape[1]
    dk = lam_q1.shape[0]
    assert s % CHUNK == 0 and gm_w_s.shape == (GM_HEADS, CHUNK, CHUNK)
    assert peer_u.shape[0] == N_KEYS * N_KEYS and peer_keys_a.shape[:2] == (PEER_HEADS, N_KEYS)

    row = lambda v: v.reshape(1, -1).astype(F32)
    x2 = x.reshape(t, d)
    bs_tile = jnp.repeat(gm_b_s.T, gm_width // GM_HEADS, axis=1)
    q_scale = (dk ** -0.5) * LOG2E
    assert s % TK == 0
    ygm, qt, k, vt3 = _proj_gmlp(x2, w_in.astype(BF16), row(gm_norm_g), row(gm_norm_b), gm_w_s, bs_tile,
                                 gm_width, da_width, q_scale)

    slopes2 = jnp.asarray(2.0 ** (-(8.0 / DA_HEADS) * np.arange(1, DA_HEADS + 1)) * LOG2E, F32)
    g_lanes = jnp.broadcast_to(da_norm_g.astype(F32)[:, :, None], da_norm_g.shape + (LANES,))
    yda = _diff_attn(qt, k.reshape(b, s, da_width), vt3, slopes2, row(lam_q1), row(lam_k1), row(lam_q2),
                     row(lam_k2), g_lanes)

    x1, x1t, sa, sb = _mix_ln_peerq(ygm, yda.reshape(t, da_width), x2, w_o.astype(BF16), row(ln1_g),
                                    row(ln1_b), peer_w_q.astype(BF16), peer_keys_a.astype(BF16),
                                    peer_keys_b.astype(BF16))
    cnt, wa, rb, wb = _peer_select(sa, sb)
    ffnt = _peer_dense(x1t, peer_u.astype(BF16), peer_v.astype(BF16).T, cnt, wa, rb, wb)
    out = _ffn_ln(ffnt, x1, row(ln2_g), row(ln2_b))
    return out.reshape(b, s, d)
```

```python
import functools
import math

import jax
import jax.numpy as jnp
import numpy as np
from jax import lax
from jax.experimental import pallas as pl
from jax.experimental.pallas import tpu as pltpu

F32 = jnp.float32
BF16 = jnp.bfloat16

GM_HEADS = 4
CHUNK = 128
DA_HEADS = 4
LAMBDA_INIT = 0.8 - 0.6 * math.exp(-0.3 * (1 - 1))
PEER_HEADS = 8
N_KEYS = 128
PEER_TOPK = 16
DEPTH = 1
ALPHA = (2.0 * DEPTH) ** 0.25
LN_EPS = 1e-5
NEG_BIG = -1e30
LOG2E = 1.4426950408889634
NOT_RANKED = 64.0

LANES = 128
SUBLANES = 8
VMEM_LIMIT = 48 * 1024 * 1024
TM_PROJ = 512
TQ = 512
SQ = 256
TK = 512
RC_ATTN = 64
TL_SEL = 512
TT_DENSE = 1024
SUB_DENSE = 256
EB_DENSE = 1024


def _gelu_exact(x):
    return 0.5 * x * (1.0 + lax.erf(x * (1.0 / math.sqrt(2.0))))


def _layer_norm_rows(x, g, b):
    mu = jnp.mean(x, axis=-1, keepdims=True)
    xc = x - mu
    var = jnp.mean(xc * xc, axis=-1, keepdims=True)
    return xc * lax.rsqrt(var + LN_EPS) * g + b


def _nt_dot(a, b):
    return lax.dot_general(a, b, (((1,), (1,)), ((), ())), preferred_element_type=F32)


def _proj_gmlp_kernel(x_ref, w_ref, wqt_ref, wvt_ref, g_ref, b_ref, ws_ref, bs_ref,
                      ygm_ref, qt_ref, k_ref, vt_ref, *, gm_width, da_width, q_scale):
    xb = x_ref[...].astype(BF16)
    tm = xb.shape[0]
    hd = gm_width // GM_HEADS
    u = _gelu_exact(jnp.dot(xb, w_ref[:, :gm_width], preferred_element_type=F32))
    z = _gelu_exact(jnp.dot(xb, w_ref[:, gm_width:2 * gm_width], preferred_element_type=F32))
    vn = _layer_norm_rows(z, g_ref[...], b_ref[...]).astype(BF16)
    row = lax.broadcasted_iota(jnp.int32, (CHUNK, CHUNK), 0)
    col = lax.broadcasted_iota(jnp.int32, (CHUNK, CHUNK), 1)
    causal = col <= row
    for hh in range(GM_HEADS):
        w_c = jnp.where(causal, ws_ref[hh], 0.0).astype(BF16)
        for c in range(tm // CHUNK):
            rs = slice(c * CHUNK, (c + 1) * CHUNK)
            cs = slice(hh * hd, (hh + 1) * hd)
            mixed = jnp.dot(w_c, vn[rs, cs], preferred_element_type=F32) + bs_ref[:, cs]
            ygm_ref[rs, cs] = (u[rs, cs] * mixed).astype(BF16)
    o = 2 * gm_width
    k_ref[...] = jnp.dot(xb, w_ref[:, o + da_width:o + 2 * da_width],
                         preferred_element_type=F32).astype(BF16)
    qt_ref[...] = (_nt_dot(wqt_ref[...], xb) * q_scale).astype(BF16)
    vt_ref[...] = _nt_dot(wvt_ref[...], xb).astype(BF16)


def _proj_gmlp(x2, w_in_b, gm_g, gm_b, w_s, bs_tile, gm_width, da_width, q_scale):
    t, d = x2.shape
    tm = TK
    cols = w_in_b.shape[1]
    o = 2 * gm_width
    wqt = w_in_b[:, o:o + da_width].T
    wvt = w_in_b[:, o + 2 * da_width:o + 3 * da_width].T
    const = lambda shape: pl.BlockSpec(shape, lambda i: (0,) * len(shape))
    row_blk = lambda w: pl.BlockSpec((tm, w), lambda i: (i, 0))
    return pl.pallas_call(
        functools.partial(_proj_gmlp_kernel, gm_width=gm_width, da_width=da_width, q_scale=q_scale),
        grid=(t // tm,),
        in_specs=[row_blk(d), const((d, cols)), const((da_width, d)), const((da_width, d)),
                  const((1, gm_width)), const((1, gm_width)),
                  const((GM_HEADS, CHUNK, CHUNK)), const((CHUNK, gm_width))],
        out_specs=[row_blk(gm_width), pl.BlockSpec((da_width, tm), lambda i: (0, i)), row_blk(da_width),
                   pl.BlockSpec((None, da_width, tm), lambda i: (i, 0, 0))],
        out_shape=[jax.ShapeDtypeStruct((t, gm_width), BF16), jax.ShapeDtypeStruct((da_width, t), BF16),
                   jax.ShapeDtypeStruct((t, da_width), BF16),
                   jax.ShapeDtypeStruct((t // tm, da_width, tm), BF16)],
        compiler_params=pltpu.CompilerParams(dimension_semantics=("arbitrary",),
                                             vmem_limit_bytes=VMEM_LIMIT),
        name="proj_gmlp",
    )(x2, w_in_b, wqt, wvt, gm_g, gm_b, w_s, bs_tile)


def _diff_attn_kernel(slope_ref, qt_ref, k_ref, vt_ref, lq1_ref, lk1_ref, lq2_ref, lk2_ref, gb_ref, o_ref,
                      s_scr, p_scr, bias_scr, acc_scr, *, tq, tk, sq, rc):
    h = pl.program_id(1)
    qi = pl.program_id(2)
    slope2 = slope_ref[h]
    dv = qt_ref.shape[0]
    dk = dv // 2
    n_sq = tq // sq
    n_ch = 2 * n_sq
    n_full = qi

    @pl.when(qi == 0)
    def _():
        kr = lax.broadcasted_iota(jnp.int32, (tk, sq), 0)
        qc = lax.broadcasted_iota(jnp.int32, (tk, sq), 1)
        bias = (kr - qc).astype(F32) * slope2
        bias_scr[0] = bias
        for si in range(n_sq):
            bias_scr[1 + si] = jnp.where(kr <= qc + si * sq, bias, NEG_BIG)

    rowq = lax.broadcasted_iota(jnp.int32, (dv, sq), 0)
    chains = []
    for si in range(n_sq):
        qt = qt_ref[:, si * sq:(si + 1) * sq]
        zero = jnp.zeros_like(qt)
        chains.append((si, jnp.where(rowq < dk, qt, zero)))
        chains.append((si, jnp.where(rowq >= dk, qt, zero)))

    def logits_to_scratch(kj, slot):
        kb = k_ref[pl.ds(pl.multiple_of(kj * tk, tk), tk), :]
        for ci, (_, qm) in enumerate(chains):
            s_scr[slot, ci] = jnp.dot(kb, qm, preferred_element_type=F32)

    def softmax_step(kj, slot, ml, diag):
        out, alphas = [], []
        for ci, (si, _) in enumerate(chains):
            m, l = ml[2 * ci], ml[2 * ci + 1]
            rel = kj * tk - (qi * tq + si * sq)
            off = rel.astype(F32) * slope2
            table = 1 + si if diag else 0

            def logits(r0):
                return s_scr[slot, ci, r0:r0 + rc, :] + bias_scr[table, r0:r0 + rc, :]

            cmax = logits(0)
            for r0 in range(rc, tk, rc):
                cmax = jnp.maximum(cmax, logits(r0))
            m_new = jnp.maximum(m, jnp.max(cmax, axis=0, keepdims=True) + off)
            shift = off - m_new
            csum = jnp.zeros((rc, sq), F32)
            for r0 in range(0, tk, rc):
                p = jnp.exp2(logits(r0) + shift)
                csum = csum + p
                p_scr[slot, ci, r0:r0 + rc, :] = p.astype(BF16)
            alpha = jnp.exp2(m - m_new)
            out += [m_new, alpha * l + jnp.sum(csum, axis=0, keepdims=True)]
            alphas.append(alpha)
        return tuple(out), alphas

    def accumulate_previous(kj_prev, slot_prev, alphas):
        vb = vt_ref[kj_prev]
        for ci in range(n_ch):
            pv = jnp.dot(vb, p_scr[slot_prev, ci], preferred_element_type=F32)
            acc_scr[ci] = alphas[ci] * (acc_scr[ci] + pv)

    def step(kj, slot, ml, prefetch_logits, diag):
        if prefetch_logits:
            logits_to_scratch(kj + 1, 1 - slot)
        ml, alphas = softmax_step(kj, slot, ml, diag)
        accumulate_previous(jnp.maximum(kj - 1, 0), 1 - slot, alphas)
        return ml

    def finalize(kj, slot, ml):
        vb = vt_ref[kj]
        lam = (jnp.exp(jnp.sum(lq1_ref[...] * lk1_ref[...], axis=1, keepdims=True))
               - jnp.exp(jnp.sum(lq2_ref[...] * lk2_ref[...], axis=1, keepdims=True)) + LAMBDA_INIT)
        g = jnp.concatenate([gb_ref[...]] * (sq // gb_ref.shape[1]), axis=1)
        for si in range(n_sq):
            a = [acc_scr[2 * si + mp] + jnp.dot(vb, p_scr[slot, 2 * si + mp], preferred_element_type=F32)
                 for mp in range(2)]
            l1, l2 = ml[4 * si + 1], ml[4 * si + 3]
            att = a[0] / l1 - lam * (a[1] / l2)
            ms = jnp.mean(att * att, axis=0, keepdims=True)
            y = att * lax.rsqrt(ms + LN_EPS) * g
            o_ref[si * sq:(si + 1) * sq, :] = (y * (1.0 - LAMBDA_INIT)).T.astype(o_ref.dtype)

    acc_scr[...] = jnp.zeros_like(acc_scr)
    p_scr[1] = jnp.zeros(p_scr.shape[1:], BF16)
    logits_to_scratch(0, 0)
    ml0 = (jnp.full((1, sq), NEG_BIG, F32), jnp.zeros((1, sq), F32)) * n_ch

    def pair(i, ml):
        ml = step(2 * i, 0, ml, True, False)
        return step(2 * i + 1, 1, ml, True, False)

    ml = lax.fori_loop(0, n_full // 2, pair, ml0)

    @pl.when(n_full % 2 == 0)
    def _():
        finalize(n_full, 0, step(n_full, 0, ml, False, True))

    @pl.when(n_full % 2 == 1)
    def _():
        ml1 = step(n_full - 1, 0, ml, True, False)
        finalize(n_full, 1, step(n_full, 1, ml1, False, True))


def _diff_attn(qt, k3, vt3, slopes2, lq1, lk1, lq2, lk2, gb):
    b, s, w = k3.shape
    dv = w // DA_HEADS
    tq, tk, sq = min(TQ, s), min(TK, s), min(SQ, s)
    assert tq == tk
    nq, nkb = s // tq, s // tk
    n_ch = 2 * (tq // sq)
    dk = lq1.shape[1]
    vec = pl.BlockSpec((1, dk), lambda bi, hi, qi: (0, 0))
    return pl.pallas_call(
        functools.partial(_diff_attn_kernel, tq=tq, tk=tk, sq=sq, rc=RC_ATTN),
        scratch_shapes=[pltpu.VMEM((2, n_ch, tk, sq), F32), pltpu.VMEM((2, n_ch, tk, sq), BF16),
                        pltpu.VMEM((1 + tq // sq, tk, sq), F32), pltpu.VMEM((n_ch, dv, sq), F32)],
        grid=(b, DA_HEADS, nq),
        in_specs=[pl.BlockSpec(memory_space=pltpu.SMEM),
                  pl.BlockSpec((dv, tq), lambda bi, hi, qi: (hi, bi * nq + qi)),
                  pl.BlockSpec((None, s, dv), lambda bi, hi, qi: (bi, 0, hi)),
                  pl.BlockSpec((nkb, dv, tk), lambda bi, hi, qi: (bi, hi, 0)),
                  vec, vec, vec, vec,
                  pl.BlockSpec((None, dv, LANES), lambda bi, hi, qi: (hi, 0, 0))],
        out_specs=pl.BlockSpec((None, tq, dv), lambda bi, hi, qi: (bi, qi, hi)),
        out_shape=jax.ShapeDtypeStruct((b, s, w), BF16),
        compiler_params=pltpu.CompilerParams(dimension_semantics=("arbitrary",) * 3,
                                             vmem_limit_bytes=VMEM_LIMIT),
        name="diff_attn",
    )(slopes2, qt, k3, vt3, lq1, lk1, lq2, lk2, gb)


def _mix_ln_kernel(ygm_ref, yda_ref, x_ref, wo_ref, g_ref, b_ref, x1_ref, x1b_ref, x1t_ref):
    gw = ygm_ref.shape[1]
    mix = (jnp.dot(ygm_ref[...], wo_ref[:gw, :], preferred_element_type=F32)
           + jnp.dot(yda_ref[...], wo_ref[gw:, :], preferred_element_type=F32))
    x1 = _layer_norm_rows(ALPHA * x_ref[...] + mix, g_ref[...], b_ref[...])
    x1_ref[...] = x1
    x1b_ref[...] = x1.astype(BF16)
    x1t_ref[...] = x1.T.astype(BF16)


def _mix_ln(ygm, yda, x2, wo_b, ln_g, ln_b):
    t, d = x2.shape
    tm = min(TM_PROJ, t)
    gw, dw = ygm.shape[1], yda.shape[1]
    const = lambda shape: pl.BlockSpec(shape, lambda i: (0,) * len(shape))
    row_blk = lambda w: pl.BlockSpec((tm, w), lambda i: (i, 0))
    return pl.pallas_call(
        _mix_ln_kernel,
        grid=(t // tm,),
        in_specs=[row_blk(gw), row_blk(dw), row_blk(d), const(wo_b.shape), const((1, d)), const((1, d))],
        out_specs=[row_blk(d), row_blk(d), pl.BlockSpec((d, tm), lambda i: (0, i))],
        out_shape=[jax.ShapeDtypeStruct((t, d), F32), jax.ShapeDtypeStruct((t, d), BF16),
                   jax.ShapeDtypeStruct((d, t), BF16)],
        compiler_params=pltpu.CompilerParams(dimension_semantics=("arbitrary",),
                                             vmem_limit_bytes=VMEM_LIMIT),
        name="mix_ln",
    )(ygm, yda, x2, wo_b, ln_g, ln_b)


def _top16_exact(val, rank_ref, vals_ref):
    key_iota = lax.broadcasted_iota(jnp.int32, val.shape, 0).astype(F32)
    rank = jnp.full(val.shape, NOT_RANKED, F32)
    for r in range(PEER_TOPK):
        m = jnp.max(val, axis=0, keepdims=True)
        idx = jnp.min(jnp.where(val == m, key_iota, float(N_KEYS)), axis=0, keepdims=True)
        hit = key_iota == idx
        rank = jnp.where(hit, float(r), rank)
        val = jnp.where(hit, -jnp.inf, val)
        vals_ref[r:r + 1, :] = m
    rank_ref[...] = rank


def _top16_if_distinct(val, rank_ref, vals_ref):
    rank = jnp.full(val.shape, NOT_RANKED, F32)
    for r in range(PEER_TOPK):
        m = jnp.max(val, axis=0, keepdims=True)
        hit = val == m
        rank = jnp.where(hit, float(r), rank)
        val = jnp.where(hit, -jnp.inf, val)
        vals_ref[r:r + 1, :] = m
    rank_ref[...] = rank
    return jnp.sum(jnp.where(val == -jnp.inf, 1.0, 0.0), axis=0, keepdims=True)


_PIECES = 10
_J_LIMIT = (8, 8, 8, 5, 4, 3, 2, 2, 2, 8)


def _candidates(va, vb):
    tl = va.shape[1]
    j8 = lax.broadcasted_iota(jnp.int32, (8, tl), 0).astype(F32)
    pieces, flats = [], []
    for p in range(_PIECES):
        if p == 0:
            c, f = vb[0:8] + va[0:1], j8
        elif p == 1:
            c, f = vb[8:16] + va[0:1], j8 + 8.0
        elif p == 9:
            c, f = va[8:16] + vb[0:1], 128.0 + 16.0 * j8
        else:
            i = p - 1
            c, f = vb[0:8] + va[i:i + 1], j8 + 16.0 * i
            c = jnp.where(j8 < float(_J_LIMIT[p]), c, -jnp.inf)
        pieces.append(c)
        flats.append(f)
    return jnp.concatenate(pieces, axis=0), jnp.concatenate(flats, axis=0)


def _joint_exact(cand, flat, s0, cnt_scr, z_scr):
    tl = cand.shape[1]
    i16 = lax.broadcasted_iota(jnp.int32, (PEER_TOPK, tl), 0).astype(F32)
    cnt = jnp.zeros((PEER_TOPK, tl), F32)
    z = jnp.zeros((1, tl), F32)
    for _ in range(PEER_TOPK):
        m = jnp.max(cand, axis=0, keepdims=True)
        idx = jnp.min(jnp.where(cand == m, flat, 1024.0), axis=0, keepdims=True)
        cand = jnp.where(flat == idx, -jnp.inf, cand)
        cnt = cnt + jnp.where(i16 == jnp.floor(idx * (1.0 / PEER_TOPK)), 1.0, 0.0)
        z = z + jnp.exp(m - s0)
    cnt_scr[...] = cnt
    z_scr[0:1, :] = z


def _joint_if_distinct(cand, s0, cnt_scr, z_scr):
    taken = jnp.zeros(cand.shape, F32)
    z = jnp.zeros((1, cand.shape[1]), F32)
    for _ in range(PEER_TOPK):
        m = jnp.max(cand, axis=0, keepdims=True)
        hit = cand == m
        taken = jnp.where(hit, 1.0, taken)
        cand = jnp.where(hit, -jnp.inf, cand)
        z = z + jnp.exp(m - s0)
    rowsum = lambda lo, hi: jnp.sum(taken[lo:hi], axis=0, keepdims=True)
    cnt_scr[0:1, :] = rowsum(0, 16)
    for i in range(1, 8):
        cnt_scr[i:i + 1, :] = rowsum(8 * (i + 1), 8 * (i + 2))
    cnt_scr[8:16, :] = taken[72:80]
    z_scr[0:1, :] = z
    return jnp.sum(taken, axis=0, keepdims=True)


def _key_scores(x1b, wq, ka, kb):
    half = ka.shape[1]
    qa = jnp.dot(x1b, wq[:, :half], preferred_element_type=F32).astype(BF16)
    qb = jnp.dot(x1b, wq[:, half:], preferred_element_type=F32).astype(BF16)
    return _nt_dot(ka, qa), _nt_dot(kb, qb)


def _peer_select_kernel(x1f_ref, wqf_ref, kaf_ref, kbf_ref, x1n_ref, wqn_ref, kan_ref, kbn_ref,
                        cnt_ref, wa_ref, rb_ref, wb_ref,
                        sc_scr, va_ref, vb_ref, ra_scr, rb_scr, cnt_scr, z_scr):
    i = pl.program_id(0)
    h = pl.program_id(1)
    slot = h % 2

    @pl.when((i == 0) & (h == 0))
    def _():
        sa0, sb0 = _key_scores(x1f_ref[...], wqf_ref[...], kaf_ref[...], kbf_ref[...])
        sc_scr[0, 0] = sa0
        sc_scr[0, 1] = sb0

    sa = sc_scr[slot, 0]
    sb = sc_scr[slot, 1]
    sa_next, sb_next = _key_scores(x1n_ref[...], wqn_ref[...], kan_ref[...], kbn_ref[...])
    sc_scr[1 - slot, 0] = sa_next
    sc_scr[1 - slot, 1] = sb_next

    ea = _top16_if_distinct(sa, ra_scr, va_ref)
    eb = _top16_if_distinct(sb, rb_scr, vb_ref)
    va = va_ref[...]
    vb = vb_ref[...]
    cand, _ = _candidates(va, vb)
    ec = _joint_if_distinct(cand, va[0:1] + vb[0:1], cnt_scr, z_scr)
    extracted = jnp.maximum(jnp.maximum(ea, eb), ec)

    @pl.when(jnp.max(extracted) > float(PEER_TOPK))
    def _():
        _top16_exact(sa, ra_scr, va_ref)
        _top16_exact(sb, rb_scr, vb_ref)
        va_x = va_ref[...]
        vb_x = vb_ref[...]
        cand_x, flat_x = _candidates(va_x, vb_x)
        _joint_exact(cand_x, flat_x, va_x[0:1] + vb_x[0:1], cnt_scr, z_scr)

    ra = ra_scr[...].astype(BF16)
    cnt = cnt_scr[...].astype(BF16)
    cnt_key = jnp.zeros(ra.shape, BF16)
    for r in range(PEER_TOPK):
        cnt_key = jnp.where(ra == float(r), cnt[r:r + 1], cnt_key)
    cnt_ref[...] = cnt_key.astype(F32)
    z = z_scr[0:1, :]
    wa_ref[...] = jnp.exp(sa - va_ref[0:1, :]) * (0.5 / z)
    rb_ref[...] = rb_scr[...].astype(BF16)
    wb_ref[...] = jnp.exp(sb - vb_ref[0:1, :]).astype(BF16)


def _peer_select(x1b, wq_b, ka_b, kb_b):
    t, d = x1b.shape
    nh, nk, half = ka_b.shape
    assert nh % 2 == 0
    tl = min(TL_SEL, t)
    nt = t // tl
    qcols = 2 * half
    out_blk = pl.BlockSpec((None, nk, tl), lambda i, h: (h, 0, i))
    out_f32 = jax.ShapeDtypeStruct((nh, nk, t), F32)
    out_b16 = jax.ShapeDtypeStruct((nh, nk, t), BF16)
    first = [pl.BlockSpec((tl, d), lambda i, h: (0, 0)), pl.BlockSpec((d, qcols), lambda i, h: (0, 0)),
             pl.BlockSpec((None, nk, half), lambda i, h: (0, 0, 0)),
             pl.BlockSpec((None, nk, half), lambda i, h: (0, 0, 0))]
    nxt = [pl.BlockSpec((tl, d), lambda i, h: (jnp.minimum(i + (h + 1) // nh, nt - 1), 0)),
           pl.BlockSpec((d, qcols), lambda i, h: (0, (h + 1) % nh)),
           pl.BlockSpec((None, nk, half), lambda i, h: ((h + 1) % nh, 0, 0)),
           pl.BlockSpec((None, nk, half), lambda i, h: ((h + 1) % nh, 0, 0))]
    return pl.pallas_call(
        _peer_select_kernel,
        grid=(nt, nh),
        in_specs=first + nxt,
        out_specs=[out_blk, out_blk, out_blk, out_blk],
        out_shape=[out_f32, out_f32, out_b16, out_b16],
        scratch_shapes=[pltpu.VMEM((2, 2, nk, tl), F32),
                        pltpu.VMEM((PEER_TOPK, tl), F32), pltpu.VMEM((PEER_TOPK, tl), F32),
                        pltpu.VMEM((nk, tl), F32), pltpu.VMEM((nk, tl), F32),
                        pltpu.VMEM((PEER_TOPK, tl), F32), pltpu.VMEM((SUBLANES, tl), F32)],
        compiler_params=pltpu.CompilerParams(dimension_semantics=("arbitrary", "arbitrary"),
                                             vmem_limit_bytes=VMEM_LIMIT),
        name="peer_select",
    )(x1b, wq_b, ka_b, kb_b, x1b, wq_b, ka_b, kb_b)


def _peer_dense_kernel(x1t_ref, u_ref, vt_ref, cnt_ref, wa_ref, rb_ref, wb_ref, o_ref, s_ref, *, sub):
    j = pl.program_id(1)
    eb = u_ref.shape[0]
    tt = x1t_ref.shape[1]
    n_sub = tt // sub
    a_rows = eb // N_KEYS
    a_grp = 2
    n_rg = N_KEYS // SUBLANES
    rg_grp = 8
    zero = jnp.zeros((), BF16)

    @pl.when(j == 0)
    def _():
        o_ref[...] = jnp.zeros_like(o_ref)

    def pre_act(c):
        s_ref[c % 2] = jnp.dot(u_ref[...], x1t_ref[:, c * sub:(c + 1) * sub], preferred_element_type=F32)

    def gate_and_activate(c):
        ts = slice(c * sub, (c + 1) * sub)
        cnt_b = [cnt_ref[hh, :, ts].astype(BF16) for hh in range(PEER_HEADS)]
        wa_b = [wa_ref[hh, :, ts].astype(BF16) for hh in range(PEER_HEADS)]
        pieces = [None] * (eb // SUBLANES)
        for a0 in range(0, a_rows, a_grp):
            for r0 in range(0, n_rg, rg_grp):
                g = [[jnp.zeros((SUBLANES, sub), BF16) for _ in range(rg_grp)] for _ in range(a_grp)]
                for hh in range(PEER_HEADS):
                    cnt = [jnp.broadcast_to(cnt_b[hh][a0 + k:a0 + k + 1], (SUBLANES, sub)) for k in range(a_grp)]
                    wa = [jnp.broadcast_to(wa_b[hh][a0 + k:a0 + k + 1], (SUBLANES, sub)) for k in range(a_grp)]
                    for r in range(rg_grp):
                        bs = slice((r0 + r) * SUBLANES, (r0 + r + 1) * SUBLANES)
                        rb = rb_ref[hh, bs, ts]
                        wb = wb_ref[hh, bs, ts]
                        for k in range(a_grp):
                            g[k][r] = g[k][r] + jnp.where(rb < cnt[k], wb * wa[k], zero)
                for k in range(a_grp):
                    for r in range(rg_grp):
                        e0 = (a0 + k) * N_KEYS + (r0 + r) * SUBLANES
                        x = s_ref[c % 2, e0:e0 + SUBLANES, :]
                        act = x * (1.0 + lax.erf(x * (1.0 / math.sqrt(2.0))))
                        pieces[e0 // SUBLANES] = act.astype(BF16) * g[k][r]
        return jnp.concatenate(pieces, axis=0)

    pre_act(0)
    for c in range(n_sub):
        if c + 1 < n_sub:
            pre_act(c + 1)
        hact = gate_and_activate(c)
        o_ref[:, c * sub:(c + 1) * sub] += jnp.dot(vt_ref[...], hact, preferred_element_type=F32)


def _peer_dense(x1t, u_b, vt_b, cnt, wa, rb, wb):
    d, t = x1t.shape
    ne = u_b.shape[0]
    tt = min(TT_DENSE, t)
    sub = min(SUB_DENSE, tt)
    eb = EB_DENSE
    a_rows = eb // N_KEYS
    row_meta = pl.BlockSpec((PEER_HEADS, a_rows, tt), lambda i, j: (0, j, i))
    key_meta = pl.BlockSpec((PEER_HEADS, N_KEYS, tt), lambda i, j: (0, 0, i))
    return pl.pallas_call(
        functools.partial(_peer_dense_kernel, sub=sub),
        grid=(t // tt, ne // eb),
        in_specs=[pl.BlockSpec((d, tt), lambda i, j: (0, i)),
                  pl.BlockSpec((eb, d), lambda i, j: (j, 0)),
                  pl.BlockSpec((d, eb), lambda i, j: (0, j)),
                  row_meta, row_meta, key_meta, key_meta],
        out_specs=pl.BlockSpec((d, tt), lambda i, j: (0, i)),
        out_shape=jax.ShapeDtypeStruct((d, t), F32),
        scratch_shapes=[pltpu.VMEM((2, eb, sub), F32)],
        compiler_params=pltpu.CompilerParams(dimension_semantics=("arbitrary", "arbitrary"),
                                             vmem_limit_bytes=VMEM_LIMIT),
        name="peer_dense",
    )(x1t, u_b, vt_b, cnt, wa, rb, wb)


def _ffn_ln_kernel(ffnt_ref, x1_ref, g_ref, b_ref, o_ref):
    ffn = ffnt_ref[...].T
    o_ref[...] = _layer_norm_rows(ALPHA * x1_ref[...] + ffn, g_ref[...], b_ref[...])


def _ffn_ln(ffnt, x1, ln_g, ln_b):
    t, d = x1.shape
    tm = min(TM_PROJ, t)
    const = lambda shape: pl.BlockSpec(shape, lambda i: (0,) * len(shape))
    return pl.pallas_call(
        _ffn_ln_kernel,
        grid=(t // tm,),
        in_specs=[pl.BlockSpec((d, tm), lambda i: (0, i)), pl.BlockSpec((tm, d), lambda i: (i, 0)),
                  const((1, d)), const((1, d))],
        out_specs=pl.BlockSpec((tm, d), lambda i: (i, 0)),
        out_shape=jax.ShapeDtypeStruct((t, d), F32),
        compiler_params=pltpu.CompilerParams(dimension_semantics=("arbitrary",),
                                             vmem_limit_bytes=VMEM_LIMIT),
        name="ffn_ln",
    )(ffnt, x1, ln_g, ln_b)


def kernel(x, w_in, gm_norm_g, gm_norm_b, gm_w_s, gm_b_s, lam_q1, lam_k1, lam_q2, lam_k2, da_norm_g,
           w_o, ln1_g, ln1_b, peer_w_q, peer_keys_a, peer_keys_b, peer_u, peer_v, ln2_g, ln2_b):
    b, s, d = x.shape
    t = b * s
    gm_width = gm_norm_g.shape[0]
    da_width = da_norm_g.shape[0] * da_norm_g.shape[1]
    dk = lam_q1.shape[0]
    assert s % CHUNK == 0 and gm_w_s.shape == (GM_HEADS, CHUNK, CHUNK)
    assert peer_u.shape[0] == N_KEYS * N_KEYS and peer_keys_a.shape[:2] == (PEER_HEADS, N_KEYS)

    row = lambda v: v.reshape(1, -1).astype(F32)
    x2 = x.reshape(t, d)
    bs_tile = jnp.repeat(gm_b_s.T, gm_width // GM_HEADS, axis=1)
    q_scale = (dk ** -0.5) * LOG2E
    assert s % TK == 0
    ygm, qt, k, vt3 = _proj_gmlp(x2, w_in.astype(BF16), row(gm_norm_g), row(gm_norm_b), gm_w_s, bs_tile,
                                 gm_width, da_width, q_scale)

    slopes2 = jnp.asarray(2.0 ** (-(8.0 / DA_HEADS) * np.arange(1, DA_HEADS + 1)) * LOG2E, F32)
    g_lanes = jnp.broadcast_to(da_norm_g.astype(F32)[:, :, None], da_norm_g.shape + (LANES,))
    yda = _diff_attn(qt, k.reshape(b, s, da_width), vt3, slopes2, row(lam_q1), row(lam_k1), row(lam_q2),
                     row(lam_k2), g_lanes)

    x1, x1b, x1t = _mix_ln(ygm, yda.reshape(t, da_width), x2, w_o.astype(BF16), row(ln1_g), row(ln1_b))
    cnt, wa, rb, wb = _peer_select(x1b, peer_w_q.astype(BF16), peer_keys_a.astype(BF16),
                                   peer_keys_b.astype(BF16))
    ffnt = _peer_dense(x1t, peer_u.astype(BF16), peer_v.astype(BF16).T, cnt, wa, rb, wb)
    out = _ffn_ln(ffnt, x1, row(ln2_g), row(ln2_b))
    return out.reshape(b, s, d)
```

```python
import functools
import math

import jax
import jax.numpy as jnp
import numpy as np
from jax import lax
from jax.experimental import pallas as pl
from jax.experimental.pallas import tpu as pltpu

F32 = jnp.float32
BF16 = jnp.bfloat16

GM_HEADS = 4
CHUNK = 128
DA_HEADS = 4
LAMBDA_INIT = 0.8 - 0.6 * math.exp(-0.3 * (1 - 1))
PEER_HEADS = 8
N_KEYS = 128
PEER_TOPK = 16
DEPTH = 1
ALPHA = (2.0 * DEPTH) ** 0.25
LN_EPS = 1e-5
NEG_BIG = -1e30
LOG2E = 1.4426950408889634
NOT_RANKED = 64.0

LANES = 128
SUBLANES = 8
VMEM_LIMIT = 48 * 1024 * 1024
TM_PROJ = 512
TQ = 512
SQ = 256
TK = 512
RC_ATTN = 64
ONES_ROWS = 16
FEAT_WIDTH = 128
TL_SEL = 512
TT_DENSE = 1024
SUB_DENSE = 256
EB_DENSE = 1024


def _gelu_exact(x):
    return 0.5 * x * (1.0 + lax.erf(x * (1.0 / math.sqrt(2.0))))


def _layer_norm_rows(x, g, b):
    mu = jnp.mean(x, axis=-1, keepdims=True)
    xc = x - mu
    var = jnp.mean(xc * xc, axis=-1, keepdims=True)
    return xc * lax.rsqrt(var + LN_EPS) * g + b


def _nt_dot(a, b):
    return lax.dot_general(a, b, (((1,), (1,)), ((), ())), preferred_element_type=F32)


def _proj_gmlp_kernel(x_ref, w_ref, wqt_ref, wvt_ref, kfeat_ref, g_ref, b_ref, ws_ref, bs_ref,
                      ygm_ref, qt_ref, k_ref, vt_ref, *, gm_width, da_width, q_scale):
    xb = x_ref[...].astype(BF16)
    tm = xb.shape[0]
    hd = gm_width // GM_HEADS
    u = _gelu_exact(jnp.dot(xb, w_ref[:, :gm_width], preferred_element_type=F32))
    z = _gelu_exact(jnp.dot(xb, w_ref[:, gm_width:2 * gm_width], preferred_element_type=F32))
    vn = _layer_norm_rows(z, g_ref[...], b_ref[...]).astype(BF16)
    row = lax.broadcasted_iota(jnp.int32, (CHUNK, CHUNK), 0)
    col = lax.broadcasted_iota(jnp.int32, (CHUNK, CHUNK), 1)
    causal = col <= row
    for hh in range(GM_HEADS):
        w_c = jnp.where(causal, ws_ref[hh], 0.0).astype(BF16)
        for c in range(tm // CHUNK):
            rs = slice(c * CHUNK, (c + 1) * CHUNK)
            cs = slice(hh * hd, (hh + 1) * hd)
            mixed = jnp.dot(w_c, vn[rs, cs], preferred_element_type=F32) + bs_ref[:, cs]
            ygm_ref[rs, cs] = (u[rs, cs] * mixed).astype(BF16)
    o = 2 * gm_width
    dvh = da_width // DA_HEADS
    fw = kfeat_ref.shape[2]
    k = jnp.dot(xb, w_ref[:, o + da_width:o + 2 * da_width], preferred_element_type=F32).astype(BF16)
    qt_ref[...] = (_nt_dot(wqt_ref[...], xb) * q_scale).astype(BF16)
    vt = _nt_dot(wvt_ref[...], xb).astype(BF16)
    ones = jnp.ones((ONES_ROWS, tm), BF16)
    for hh in range(DA_HEADS):
        k_ref[:, hh * (dvh + fw):hh * (dvh + fw) + dvh] = k[:, hh * dvh:(hh + 1) * dvh]
        k_ref[:, hh * (dvh + fw) + dvh:(hh + 1) * (dvh + fw)] = kfeat_ref[hh]
        r0 = hh * (dvh + ONES_ROWS)
        vt_ref[r0:r0 + dvh, :] = vt[hh * dvh:(hh + 1) * dvh]
        vt_ref[r0 + dvh:r0 + dvh + ONES_ROWS, :] = ones


def _proj_gmlp(x2, w_in_b, kfeat, gm_g, gm_b, w_s, bs_tile, gm_width, da_width, q_scale):
    t, d = x2.shape
    tm = TK
    cols = w_in_b.shape[1]
    kw = da_width + DA_HEADS * kfeat.shape[2]
    vr = da_width + DA_HEADS * ONES_ROWS
    o = 2 * gm_width
    wqt = w_in_b[:, o:o + da_width].T
    wvt = w_in_b[:, o + 2 * da_width:o + 3 * da_width].T
    const = lambda shape: pl.BlockSpec(shape, lambda i: (0,) * len(shape))
    row_blk = lambda w: pl.BlockSpec((tm, w), lambda i: (i, 0))
    return pl.pallas_call(
        functools.partial(_proj_gmlp_kernel, gm_width=gm_width, da_width=da_width, q_scale=q_scale),
        grid=(t // tm,),
        in_specs=[row_blk(d), const((d, cols)), const((da_width, d)), const((da_width, d)), const(kfeat.shape),
                  const((1, gm_width)), const((1, gm_width)),
                  const((GM_HEADS, CHUNK, CHUNK)), const((CHUNK, gm_width))],
        out_specs=[row_blk(gm_width), pl.BlockSpec((da_width, tm), lambda i: (0, i)), row_blk(kw),
                   pl.BlockSpec((None, vr, tm), lambda i: (i, 0, 0))],
        out_shape=[jax.ShapeDtypeStruct((t, gm_width), BF16), jax.ShapeDtypeStruct((da_width, t), BF16),
                   jax.ShapeDtypeStruct((t, kw), BF16),
                   jax.ShapeDtypeStruct((t // tm, vr, tm), BF16)],
        compiler_params=pltpu.CompilerParams(dimension_semantics=("arbitrary",),
                                             vmem_limit_bytes=VMEM_LIMIT),
        name="proj_gmlp",
    )(x2, w_in_b, wqt, wvt, kfeat, gm_g, gm_b, w_s, bs_tile)


def _diff_attn_kernel(slope_ref, qt_ref, qaug_ref, k_ref, vt_ref, lq1_ref, lk1_ref, lq2_ref, lk2_ref, gb_ref, o_ref,
                      s_scr, p_scr, mask_scr, acc_scr, *, tq, tk, sq, rc):
    h = pl.program_id(1)
    qi = pl.program_id(2)
    slope2 = slope_ref[h]
    dv = qt_ref.shape[0]
    dk = dv // 2
    n_sq = tq // sq
    n_ch = 2 * n_sq
    n_full = qi

    @pl.when(qi == 0)
    def _():
        kr = lax.broadcasted_iota(jnp.int32, (tk, sq), 0)
        qc = lax.broadcasted_iota(jnp.int32, (tk, sq), 1)
        for si in range(n_sq):
            mask_scr[si] = jnp.where(kr <= qc + si * sq, 0.0, NEG_BIG)

    rowq = lax.broadcasted_iota(jnp.int32, (dv, sq), 0)
    chains = []
    for si in range(n_sq):
        qt = qt_ref[:, si * sq:(si + 1) * sq]
        zero = jnp.zeros_like(qt)
        chains.append((si, jnp.concatenate([jnp.where(rowq < dk, qt, zero), qaug_ref[...]], axis=0)))
        chains.append((si, jnp.concatenate([jnp.where(rowq >= dk, qt, zero), qaug_ref[...]], axis=0)))

    def logits_to_scratch(kj, slot):
        kb = k_ref[pl.ds(pl.multiple_of(kj * tk, tk), tk), :]
        for ci, (_, qm) in enumerate(chains):
            s_scr[slot, ci] = jnp.dot(kb, qm, preferred_element_type=F32)

    def softmax_step(kj, slot, ml, diag):
        out, alphas = [], []
        for ci, (si, _) in enumerate(chains):
            m = ml[ci]
            rel = kj * tk - (qi * tq + si * sq)
            off = rel.astype(F32) * slope2

            def logits(r0):
                t = s_scr[slot, ci, r0:r0 + rc, :]
                return t + mask_scr[si, r0:r0 + rc, :] if diag else t

            cmax = logits(0)
            for r0 in range(rc, tk, rc):
                cmax = jnp.maximum(cmax, logits(r0))
            m_new = jnp.maximum(m, jnp.max(cmax, axis=0, keepdims=True) + off)
            shift = off - m_new
            for r0 in range(0, tk, rc):
                p_scr[slot, ci, r0:r0 + rc, :] = jnp.exp2(logits(r0) + shift).astype(BF16)
            out.append(m_new)
            alphas.append(jnp.exp2(m - m_new))
        return tuple(out), alphas

    def accumulate_previous(kj_prev, slot_prev, alphas):
        vb = vt_ref[kj_prev]
        for ci in range(n_ch):
            pv = jnp.dot(vb, p_scr[slot_prev, ci], preferred_element_type=F32)
            acc_scr[ci] = alphas[ci] * (acc_scr[ci] + pv)

    def step(kj, slot, ml, prefetch_logits, diag):
        if prefetch_logits:
            logits_to_scratch(kj + 1, 1 - slot)
        ml, alphas = softmax_step(kj, slot, ml, diag)
        accumulate_previous(jnp.maximum(kj - 1, 0), 1 - slot, alphas)
        return ml

    def finalize(kj, slot, ml):
        vb = vt_ref[kj]
        lam = (jnp.exp(jnp.sum(lq1_ref[...] * lk1_ref[...], axis=1, keepdims=True))
               - jnp.exp(jnp.sum(lq2_ref[...] * lk2_ref[...], axis=1, keepdims=True)) + LAMBDA_INIT)
        g = jnp.concatenate([gb_ref[...]] * (sq // gb_ref.shape[1]), axis=1)
        for si in range(n_sq):
            a = [acc_scr[2 * si + mp] + jnp.dot(vb, p_scr[slot, 2 * si + mp], preferred_element_type=F32)
                 for mp in range(2)]
            l1, l2 = a[0][dv:dv + 1], a[1][dv:dv + 1]
            att = a[0][:dv] / l1 - lam * (a[1][:dv] / l2)
            ms = jnp.mean(att * att, axis=0, keepdims=True)
            y = att * lax.rsqrt(ms + LN_EPS) * g
            o_ref[si * sq:(si + 1) * sq, :] = (y * (1.0 - LAMBDA_INIT)).T.astype(o_ref.dtype)

    acc_scr[...] = jnp.zeros_like(acc_scr)
    p_scr[1] = jnp.zeros(p_scr.shape[1:], BF16)
    logits_to_scratch(0, 0)
    ml0 = (jnp.full((1, sq), NEG_BIG, F32),) * n_ch

    def pair(i, ml):
        ml = step(2 * i, 0, ml, True, False)
        return step(2 * i + 1, 1, ml, True, False)

    ml = lax.fori_loop(0, n_full // 2, pair, ml0)

    @pl.when(n_full % 2 == 0)
    def _():
        finalize(n_full, 0, step(n_full, 0, ml, False, True))

    @pl.when(n_full % 2 == 1)
    def _():
        ml1 = step(n_full - 1, 0, ml, True, False)
        finalize(n_full, 1, step(n_full, 1, ml1, False, True))


def _diff_attn(qt, qaug, k3, vt3, slopes2, lq1, lk1, lq2, lk2, gb):
    b, s, _ = k3.shape
    dv = qt.shape[0] // DA_HEADS
    kw = k3.shape[2] // DA_HEADS
    dva = vt3.shape[1] // DA_HEADS
    w = dv * DA_HEADS
    tq, tk, sq = min(TQ, s), min(TK, s), min(SQ, s)
    assert tq == tk
    nq, nkb = s // tq, s // tk
    n_ch = 2 * (tq // sq)
    dk = lq1.shape[1]
    vec = pl.BlockSpec((1, dk), lambda bi, hi, qi: (0, 0))
    return pl.pallas_call(
        functools.partial(_diff_attn_kernel, tq=tq, tk=tk, sq=sq, rc=RC_ATTN),
        scratch_shapes=[pltpu.VMEM((2, n_ch, tk, sq), F32), pltpu.VMEM((2, n_ch, tk, sq), BF16),
                        pltpu.VMEM((tq // sq, tk, sq), F32), pltpu.VMEM((n_ch, dva, sq), F32)],
        grid=(b, DA_HEADS, nq),
        in_specs=[pl.BlockSpec(memory_space=pltpu.SMEM),
                  pl.BlockSpec((dv, tq), lambda bi, hi, qi: (hi, bi * nq + qi)),
                  pl.BlockSpec((None, kw - dv, sq), lambda bi, hi, qi: (hi, 0, 0)),
                  pl.BlockSpec((None, s, kw), lambda bi, hi, qi: (bi, 0, hi)),
                  pl.BlockSpec((nkb, dva, tk), lambda bi, hi, qi: (bi, hi, 0)),
                  vec, vec, vec, vec,
                  pl.BlockSpec((None, dv, LANES), lambda bi, hi, qi: (hi, 0, 0))],
        out_specs=pl.BlockSpec((None, tq, dv), lambda bi, hi, qi: (bi, qi, hi)),
        out_shape=jax.ShapeDtypeStruct((b, s, w), BF16),
        compiler_params=pltpu.CompilerParams(dimension_semantics=("arbitrary",) * 3,
                                             vmem_limit_bytes=VMEM_LIMIT),
        name="diff_attn",
    )(slopes2, qt, qaug, k3, vt3, lq1, lk1, lq2, lk2, gb)


def _mix_ln_kernel(ygm_ref, yda_ref, x_ref, wo_ref, g_ref, b_ref, x1_ref, x1b_ref, x1t_ref):
    gw = ygm_ref.shape[1]
    mix = (jnp.dot(ygm_ref[...], wo_ref[:gw, :], preferred_element_type=F32)
           + jnp.dot(yda_ref[...], wo_ref[gw:, :], preferred_element_type=F32))
    x1 = _layer_norm_rows(ALPHA * x_ref[...] + mix, g_ref[...], b_ref[...])
    x1_ref[...] = x1
    x1b_ref[...] = x1.astype(BF16)
    x1t_ref[...] = x1.T.astype(BF16)


def _mix_ln(ygm, yda, x2, wo_b, ln_g, ln_b):
    t, d = x2.shape
    tm = min(TM_PROJ, t)
    gw, dw = ygm.shape[1], yda.shape[1]
    const = lambda shape: pl.BlockSpec(shape, lambda i: (0,) * len(shape))
    row_blk = lambda w: pl.BlockSpec((tm, w), lambda i: (i, 0))
    return pl.pallas_call(
        _mix_ln_kernel,
        grid=(t // tm,),
        in_specs=[row_blk(gw), row_blk(dw), row_blk(d), const(wo_b.shape), const((1, d)), const((1, d))],
        out_specs=[row_blk(d), row_blk(d), pl.BlockSpec((d, tm), lambda i: (0, i))],
        out_shape=[jax.ShapeDtypeStruct((t, d), F32), jax.ShapeDtypeStruct((t, d), BF16),
                   jax.ShapeDtypeStruct((d, t), BF16)],
        compiler_params=pltpu.CompilerParams(dimension_semantics=("arbitrary",),
                                             vmem_limit_bytes=VMEM_LIMIT),
        name="mix_ln",
    )(ygm, yda, x2, wo_b, ln_g, ln_b)


def _top16_exact(val, rank_ref, vals_ref):
    key_iota = lax.broadcasted_iota(jnp.int32, val.shape, 0).astype(F32)
    rank = jnp.full(val.shape, NOT_RANKED, F32)
    for r in range(PEER_TOPK):
        m = jnp.max(val, axis=0, keepdims=True)
        idx = jnp.min(jnp.where(val == m, key_iota, float(N_KEYS)), axis=0, keepdims=True)
        hit = key_iota == idx
        rank = jnp.where(hit, float(r), rank)
        val = jnp.where(hit, -jnp.inf, val)
        vals_ref[r:r + 1, :] = m
    rank_ref[...] = rank


def _top16_if_distinct(val, rank_ref, vals_ref):
    rank = jnp.full(val.shape, NOT_RANKED, F32)
    for r in range(PEER_TOPK):
        m = jnp.max(val, axis=0, keepdims=True)
        hit = val == m
        rank = jnp.where(hit, float(r), rank)
        val = jnp.where(hit, -jnp.inf, val)
        vals_ref[r:r + 1, :] = m
    rank_ref[...] = rank
    return jnp.sum(jnp.where(val == -jnp.inf, 1.0, 0.0), axis=0, keepdims=True)


_PIECES = 10
_J_LIMIT = (8, 8, 8, 5, 4, 3, 2, 2, 2, 8)


def _candidates(va, vb):
    tl = va.shape[1]
    j8 = lax.broadcasted_iota(jnp.int32, (8, tl), 0).astype(F32)
    pieces, flats = [], []
    for p in range(_PIECES):
        if p == 0:
            c, f = vb[0:8] + va[0:1], j8
        elif p == 1:
            c, f = vb[8:16] + va[0:1], j8 + 8.0
        elif p == 9:
            c, f = va[8:16] + vb[0:1], 128.0 + 16.0 * j8
        else:
            i = p - 1
            c, f = vb[0:8] + va[i:i + 1], j8 + 16.0 * i
            c = jnp.where(j8 < float(_J_LIMIT[p]), c, -jnp.inf)
        pieces.append(c)
        flats.append(f)
    return jnp.concatenate(pieces, axis=0), jnp.concatenate(flats, axis=0)


def _joint_exact(cand, flat, s0, cnt_scr, z_scr):
    tl = cand.shape[1]
    i16 = lax.broadcasted_iota(jnp.int32, (PEER_TOPK, tl), 0).astype(F32)
    cnt = jnp.zeros((PEER_TOPK, tl), F32)
    z = jnp.zeros((1, tl), F32)
    for _ in range(PEER_TOPK):
        m = jnp.max(cand, axis=0, keepdims=True)
        idx = jnp.min(jnp.where(cand == m, flat, 1024.0), axis=0, keepdims=True)
        cand = jnp.where(flat == idx, -jnp.inf, cand)
        cnt = cnt + jnp.where(i16 == jnp.floor(idx * (1.0 / PEER_TOPK)), 1.0, 0.0)
        z = z + jnp.exp(m - s0)
    cnt_scr[...] = cnt
    z_scr[0:1, :] = z


def _joint_if_distinct(cand, s0, cnt_scr, z_scr):
    taken = jnp.zeros(cand.shape, F32)
    z = jnp.zeros((1, cand.shape[1]), F32)
    for _ in range(PEER_TOPK):
        m = jnp.max(cand, axis=0, keepdims=True)
        hit = cand == m
        taken = jnp.where(hit, 1.0, taken)
        cand = jnp.where(hit, -jnp.inf, cand)
        z = z + jnp.exp(m - s0)
    rowsum = lambda lo, hi: jnp.sum(taken[lo:hi], axis=0, keepdims=True)
    cnt_scr[0:1, :] = rowsum(0, 16)
    for i in range(1, 8):
        cnt_scr[i:i + 1, :] = rowsum(8 * (i + 1), 8 * (i + 2))
    cnt_scr[8:16, :] = taken[72:80]
    z_scr[0:1, :] = z
    return jnp.sum(taken, axis=0, keepdims=True)


def _key_scores(x1b, wq, ka, kb):
    half = ka.shape[1]
    qa = jnp.dot(x1b, wq[:, :half], preferred_element_type=F32).astype(BF16)
    qb = jnp.dot(x1b, wq[:, half:], preferred_element_type=F32).astype(BF16)
    return _nt_dot(ka, qa), _nt_dot(kb, qb)


def _peer_select_kernel(x1f_ref, wqf_ref, kaf_ref, kbf_ref, x1n_ref, wqn_ref, kan_ref, kbn_ref,
                        cnt_ref, wa_ref, rb_ref, wb_ref,
                        sc_scr, va_ref, vb_ref, ra_scr, rb_scr, cnt_scr, z_scr):
    i = pl.program_id(0)
    h = pl.program_id(1)
    slot = h % 2

    @pl.when((i == 0) & (h == 0))
    def _():
        sa0, sb0 = _key_scores(x1f_ref[...], wqf_ref[...], kaf_ref[...], kbf_ref[...])
        sc_scr[0, 0] = sa0
        sc_scr[0, 1] = sb0

    sa = sc_scr[slot, 0]
    sb = sc_scr[slot, 1]
    sa_next, sb_next = _key_scores(x1n_ref[...], wqn_ref[...], kan_ref[...], kbn_ref[...])
    sc_scr[1 - slot, 0] = sa_next
    sc_scr[1 - slot, 1] = sb_next

    ea = _top16_if_distinct(sa, ra_scr, va_ref)
    eb = _top16_if_distinct(sb, rb_scr, vb_ref)
    va = va_ref[...]
    vb = vb_ref[...]
    cand, _ = _candidates(va, vb)
    ec = _joint_if_distinct(cand, va[0:1] + vb[0:1], cnt_scr, z_scr)
    extracted = jnp.maximum(jnp.maximum(ea, eb), ec)

    @pl.when(jnp.max(extracted) > float(PEER_TOPK))
    def _():
        _top16_exact(sa, ra_scr, va_ref)
        _top16_exact(sb, rb_scr, vb_ref)
        va_x = va_ref[...]
        vb_x = vb_ref[...]
        cand_x, flat_x = _candidates(va_x, vb_x)
        _joint_exact(cand_x, flat_x, va_x[0:1] + vb_x[0:1], cnt_scr, z_scr)

    ra = ra_scr[...].astype(BF16)
    cnt = cnt_scr[...].astype(BF16)
    cnt_key = jnp.zeros(ra.shape, BF16)
    for r in range(PEER_TOPK):
        cnt_key = jnp.where(ra == float(r), cnt[r:r + 1], cnt_key)
    cnt_ref[...] = cnt_key.astype(F32)
    z = z_scr[0:1, :]
    wa_ref[...] = jnp.exp(sa - va_ref[0:1, :]) * (0.5 / z)
    rb_ref[...] = rb_scr[...].astype(BF16)
    wb_ref[...] = jnp.exp(sb - vb_ref[0:1, :]).astype(BF16)


def _peer_select(x1b, wq_b, ka_b, kb_b):
    t, d = x1b.shape
    nh, nk, half = ka_b.shape
    assert nh % 2 == 0
    tl = min(TL_SEL, t)
    nt = t // tl
    qcols = 2 * half
    out_blk = pl.BlockSpec((None, nk, tl), lambda i, h: (h, 0, i))
    out_f32 = jax.ShapeDtypeStruct((nh, nk, t), F32)
    out_b16 = jax.ShapeDtypeStruct((nh, nk, t), BF16)
    first = [pl.BlockSpec((tl, d), lambda i, h: (0, 0)), pl.BlockSpec((d, qcols), lambda i, h: (0, 0)),
             pl.BlockSpec((None, nk, half), lambda i, h: (0, 0, 0)),
             pl.BlockSpec((None, nk, half), lambda i, h: (0, 0, 0))]
    nxt = [pl.BlockSpec((tl, d), lambda i, h: (jnp.minimum(i + (h + 1) // nh, nt - 1), 0)),
           pl.BlockSpec((d, qcols), lambda i, h: (0, (h + 1) % nh)),
           pl.BlockSpec((None, nk, half), lambda i, h: ((h + 1) % nh, 0, 0)),
           pl.BlockSpec((None, nk, half), lambda i, h: ((h + 1) % nh, 0, 0))]
    return pl.pallas_call(
        _peer_select_kernel,
        grid=(nt, nh),
        in_specs=first + nxt,
        out_specs=[out_blk, out_blk, out_blk, out_blk],
        out_shape=[out_f32, out_f32, out_b16, out_b16],
        scratch_shapes=[pltpu.VMEM((2, 2, nk, tl), F32),
                        pltpu.VMEM((PEER_TOPK, tl), F32), pltpu.VMEM((PEER_TOPK, tl), F32),
                        pltpu.VMEM((nk, tl), F32), pltpu.VMEM((nk, tl), F32),
                        pltpu.VMEM((PEER_TOPK, tl), F32), pltpu.VMEM((SUBLANES, tl), F32)],
        compiler_params=pltpu.CompilerParams(dimension_semantics=("arbitrary", "arbitrary"),
                                             vmem_limit_bytes=VMEM_LIMIT),
        name="peer_select",
    )(x1b, wq_b, ka_b, kb_b, x1b, wq_b, ka_b, kb_b)


def _peer_dense_kernel(x1t_ref, u_ref, vt_ref, cnt_ref, wa_ref, rb_ref, wb_ref, o_ref, s_ref, *, sub):
    j = pl.program_id(1)
    eb = u_ref.shape[0]
    tt = x1t_ref.shape[1]
    n_sub = tt // sub
    a_rows = eb // N_KEYS
    a_grp = 2
    n_rg = N_KEYS // SUBLANES
    rg_grp = 8
    zero = jnp.zeros((), BF16)

    @pl.when(j == 0)
    def _():
        o_ref[...] = jnp.zeros_like(o_ref)

    def pre_act(c):
        s_ref[c % 2] = jnp.dot(u_ref[...], x1t_ref[:, c * sub:(c + 1) * sub], preferred_element_type=F32)

    def gate_and_activate(c):
        ts = slice(c * sub, (c + 1) * sub)
        cnt_b = [cnt_ref[hh, :, ts].astype(BF16) for hh in range(PEER_HEADS)]
        wa_b = [wa_ref[hh, :, ts].astype(BF16) for hh in range(PEER_HEADS)]
        pieces = [None] * (eb // SUBLANES)
        for a0 in range(0, a_rows, a_grp):
            for r0 in range(0, n_rg, rg_grp):
                g = [[jnp.zeros((SUBLANES, sub), BF16) for _ in range(rg_grp)] for _ in range(a_grp)]
                for hh in range(PEER_HEADS):
                    cnt = [jnp.broadcast_to(cnt_b[hh][a0 + k:a0 + k + 1], (SUBLANES, sub)) for k in range(a_grp)]
                    wa = [jnp.broadcast_to(wa_b[hh][a0 + k:a0 + k + 1], (SUBLANES, sub)) for k in range(a_grp)]
                    for r in range(rg_grp):
                        bs = slice((r0 + r) * SUBLANES, (r0 + r + 1) * SUBLANES)
                        rb = rb_ref[hh, bs, ts]
                        wb = wb_ref[hh, bs, ts]
                        for k in range(a_grp):
                            g[k][r] = g[k][r] + jnp.where(rb < cnt[k], wb * wa[k], zero)
                for k in range(a_grp):
                    for r in range(rg_grp):
                        e0 = (a0 + k) * N_KEYS + (r0 + r) * SUBLANES
                        x = s_ref[c % 2, e0:e0 + SUBLANES, :]
                        act = x * (1.0 + lax.erf(x * (1.0 / math.sqrt(2.0))))
                        pieces[e0 // SUBLANES] = act.astype(BF16) * g[k][r]
        return jnp.concatenate(pieces, axis=0)

    pre_act(0)
    for c in range(n_sub):
        if c + 1 < n_sub:
            pre_act(c + 1)
        hact = gate_and_activate(c)
        o_ref[:, c * sub:(c + 1) * sub] += jnp.dot(vt_ref[...], hact, preferred_element_type=F32)


def _peer_dense(x1t, u_b, vt_b, cnt, wa, rb, wb):
    d, t = x1t.shape
    ne = u_b.shape[0]
    tt = min(TT_DENSE, t)
    sub = min(SUB_DENSE, tt)
    eb = EB_DENSE
    a_rows = eb // N_KEYS
    row_meta = pl.BlockSpec((PEER_HEADS, a_rows, tt), lambda i, j: (0, j, i))
    key_meta = pl.BlockSpec((PEER_HEADS, N_KEYS, tt), lambda i, j: (0, 0, i))
    return pl.pallas_call(
        functools.partial(_peer_dense_kernel, sub=sub),
        grid=(t // tt, ne // eb),
        in_specs=[pl.BlockSpec((d, tt), lambda i, j: (0, i)),
                  pl.BlockSpec((eb, d), lambda i, j: (j, 0)),
                  pl.BlockSpec((d, eb), lambda i, j: (0, j)),
                  row_meta, row_meta, key_meta, key_meta],
        out_specs=pl.BlockSpec((d, tt), lambda i, j: (0, i)),
        out_shape=jax.ShapeDtypeStruct((d, t), F32),
        scratch_shapes=[pltpu.VMEM((2, eb, sub), F32)],
        compiler_params=pltpu.CompilerParams(dimension_semantics=("arbitrary", "arbitrary"),
                                             vmem_limit_bytes=VMEM_LIMIT),
        name="peer_dense",
    )(x1t, u_b, vt_b, cnt, wa, rb, wb)


def _ffn_ln_kernel(ffnt_ref, x1_ref, g_ref, b_ref, o_ref):
    ffn = ffnt_ref[...].T
    o_ref[...] = _layer_norm_rows(ALPHA * x1_ref[...] + ffn, g_ref[...], b_ref[...])


def _ffn_ln(ffnt, x1, ln_g, ln_b):
    t, d = x1.shape
    tm = min(TM_PROJ, t)
    const = lambda shape: pl.BlockSpec(shape, lambda i: (0,) * len(shape))
    return pl.pallas_call(
        _ffn_ln_kernel,
        grid=(t // tm,),
        in_specs=[pl.BlockSpec((d, tm), lambda i: (0, i)), pl.BlockSpec((tm, d), lambda i: (i, 0)),
                  const((1, d)), const((1, d))],
        out_specs=pl.BlockSpec((tm, d), lambda i: (i, 0)),
        out_shape=jax.ShapeDtypeStruct((t, d), F32),
        compiler_params=pltpu.CompilerParams(dimension_semantics=("arbitrary",),
                                             vmem_limit_bytes=VMEM_LIMIT),
        name="ffn_ln",
    )(ffnt, x1, ln_g, ln_b)


def _alibi_features(slopes2, tk, sq):
    assert tk <= 512 and sq <= 256
    c = jnp.asarray(slopes2, F32)
    nh = c.shape[0]
    hi = c.astype(BF16).astype(F32)
    mid = (c - hi).astype(BF16).astype(F32)
    lo = (c - hi - mid).astype(BF16).astype(F32)
    parts = jnp.stack([hi, mid, lo], axis=1)
    kr = np.arange(tk)
    kfeat = jnp.zeros((nh, tk, FEAT_WIDTH), F32)
    kfeat = kfeat.at[:, :, 0:3].set((kr % 256).astype(np.float32)[None, :, None])
    kfeat = kfeat.at[:, :, 3:6].set((256 * (kr // 256)).astype(np.float32)[None, :, None])
    kfeat = kfeat.at[:, :, 6:9].set(jnp.broadcast_to(parts[:, None, :], (nh, tk, 3)))
    qfeat = jnp.zeros((nh, FEAT_WIDTH, sq), F32)
    qfeat = qfeat.at[:, 0:3, :].set(jnp.broadcast_to(parts[:, :, None], (nh, 3, sq)))
    qfeat = qfeat.at[:, 3:6, :].set(jnp.broadcast_to(parts[:, :, None], (nh, 3, sq)))
    qfeat = qfeat.at[:, 6:9, :].set(jnp.broadcast_to(-jnp.arange(sq, dtype=F32)[None, None, :], (nh, 3, sq)))
    return kfeat.astype(BF16), qfeat.astype(BF16)


def kernel(x, w_in, gm_norm_g, gm_norm_b, gm_w_s, gm_b_s, lam_q1, lam_k1, lam_q2, lam_k2, da_norm_g,
           w_o, ln1_g, ln1_b, peer_w_q, peer_keys_a, peer_keys_b, peer_u, peer_v, ln2_g, ln2_b):
    b, s, d = x.shape
    t = b * s
    gm_width = gm_norm_g.shape[0]
    da_width = da_norm_g.shape[0] * da_norm_g.shape[1]
    dk = lam_q1.shape[0]
    assert s % CHUNK == 0 and gm_w_s.shape == (GM_HEADS, CHUNK, CHUNK)
    assert peer_u.shape[0] == N_KEYS * N_KEYS and peer_keys_a.shape[:2] == (PEER_HEADS, N_KEYS)

    row = lambda v: v.reshape(1, -1).astype(F32)
    x2 = x.reshape(t, d)
    bs_tile = jnp.repeat(gm_b_s.T, gm_width // GM_HEADS, axis=1)
    q_scale = (dk ** -0.5) * LOG2E
    assert s % TK == 0
    slopes2 = jnp.asarray(2.0 ** (-(8.0 / DA_HEADS) * np.arange(1, DA_HEADS + 1)) * LOG2E, F32)
    kfeat, qfeat = _alibi_features(slopes2, TK, SQ)
    ygm, qt, k, vt3 = _proj_gmlp(x2, w_in.astype(BF16), kfeat, row(gm_norm_g), row(gm_norm_b), gm_w_s,
                                 bs_tile, gm_width, da_width, q_scale)

    g_lanes = jnp.broadcast_to(da_norm_g.astype(F32)[:, :, None], da_norm_g.shape + (LANES,))
    yda = _diff_attn(qt, qfeat, k.reshape(b, s, -1), vt3, slopes2, row(lam_q1), row(lam_k1), row(lam_q2),
                     row(lam_k2), g_lanes)

    x1, x1b, x1t = _mix_ln(ygm, yda.reshape(t, da_width), x2, w_o.astype(BF16), row(ln1_g), row(ln1_b))
    cnt, wa, rb, wb = _peer_select(x1b, peer_w_q.astype(BF16), peer_keys_a.astype(BF16),
                                   peer_keys_b.astype(BF16))
    ffnt = _peer_dense(x1t, peer_u.astype(BF16), peer_v.astype(BF16).T, cnt, wa, rb, wb)
    out = _ffn_ln(ffnt, x1, row(ln2_g), row(ln2_b))
    return out.reshape(b, s, d)
```

```python
import functools
import math

import jax
import jax.numpy as jnp
import numpy as np
from jax import lax
from jax.experimental import pallas as pl
from jax.experimental.pallas import tpu as pltpu

F32 = jnp.float32
BF16 = jnp.bfloat16

GM_HEADS = 4
CHUNK = 128
DA_HEADS = 4
LAMBDA_INIT = 0.8 - 0.6 * math.exp(-0.3 * (1 - 1))
PEER_HEADS = 8
N_KEYS = 128
PEER_TOPK = 16
DEPTH = 1
ALPHA = (2.0 * DEPTH) ** 0.25
LN_EPS = 1e-5
NEG_BIG = -1e30
LOG2E = 1.4426950408889634
NOT_RANKED = 64.0

LANES = 128
SUBLANES = 8
VMEM_LIMIT = 48 * 1024 * 1024
TM_PROJ = 512
TQ = 512
SQ = 256
TK = 512
RC_ATTN = 64
ONES_ROWS = 16
FEAT_WIDTH = 128
TL_SEL = 512
TT_DENSE = 1024
SUB_DENSE = 1024
EB_DENSE = 1024


def _gelu_exact(x):
    return 0.5 * x * (1.0 + lax.erf(x * (1.0 / math.sqrt(2.0))))


def _layer_norm_rows(x, g, b):
    mu = jnp.mean(x, axis=-1, keepdims=True)
    xc = x - mu
    var = jnp.mean(xc * xc, axis=-1, keepdims=True)
    return xc * lax.rsqrt(var + LN_EPS) * g + b


def _nt_dot(a, b):
    return lax.dot_general(a, b, (((1,), (1,)), ((), ())), preferred_element_type=F32)


def _proj_gmlp_kernel(x_ref, w_ref, wqt_ref, wvt_ref, kfeat_ref, g_ref, b_ref, ws_ref, bs_ref,
                      ygm_ref, qt_ref, k_ref, vt_ref, *, gm_width, da_width, q_scale):
    xb = x_ref[...].astype(BF16)
    tm = xb.shape[0]
    hd = gm_width // GM_HEADS
    u = _gelu_exact(jnp.dot(xb, w_ref[:, :gm_width], preferred_element_type=F32))
    z = _gelu_exact(jnp.dot(xb, w_ref[:, gm_width:2 * gm_width], preferred_element_type=F32))
    vn = _layer_norm_rows(z, g_ref[...], b_ref[...]).astype(BF16)
    row = lax.broadcasted_iota(jnp.int32, (CHUNK, CHUNK), 0)
    col = lax.broadcasted_iota(jnp.int32, (CHUNK, CHUNK), 1)
    causal = col <= row
    for hh in range(GM_HEADS):
        w_c = jnp.where(causal, ws_ref[hh], 0.0).astype(BF16)
        for c in range(tm // CHUNK):
            rs = slice(c * CHUNK, (c + 1) * CHUNK)
            cs = slice(hh * hd, (hh + 1) * hd)
            mixed = jnp.dot(w_c, vn[rs, cs], preferred_element_type=F32) + bs_ref[:, cs]
            ygm_ref[rs, cs] = (u[rs, cs] * mixed).astype(BF16)
    o = 2 * gm_width
    dvh = da_width // DA_HEADS
    fw = kfeat_ref.shape[2]
    k = jnp.dot(xb, w_ref[:, o + da_width:o + 2 * da_width], preferred_element_type=F32).astype(BF16)
    qt_ref[...] = (_nt_dot(wqt_ref[...], xb) * q_scale).astype(BF16)
    vt = _nt_dot(wvt_ref[...], xb).astype(BF16)
    ones = jnp.ones((ONES_ROWS, tm), BF16)
    for hh in range(DA_HEADS):
        k_ref[:, hh * (dvh + fw):hh * (dvh + fw) + dvh] = k[:, hh * dvh:(hh + 1) * dvh]
        k_ref[:, hh * (dvh + fw) + dvh:(hh + 1) * (dvh + fw)] = kfeat_ref[hh]
        r0 = hh * (dvh + ONES_ROWS)
        vt_ref[r0:r0 + dvh, :] = vt[hh * dvh:(hh + 1) * dvh]
        vt_ref[r0 + dvh:r0 + dvh + ONES_ROWS, :] = ones


def _proj_gmlp(x2, w_in_b, kfeat, gm_g, gm_b, w_s, bs_tile, gm_width, da_width, q_scale):
    t, d = x2.shape
    tm = TK
    cols = w_in_b.shape[1]
    kw = da_width + DA_HEADS * kfeat.shape[2]
    vr = da_width + DA_HEADS * ONES_ROWS
    o = 2 * gm_width
    wqt = w_in_b[:, o:o + da_width].T
    wvt = w_in_b[:, o + 2 * da_width:o + 3 * da_width].T
    const = lambda shape: pl.BlockSpec(shape, lambda i: (0,) * len(shape))
    row_blk = lambda w: pl.BlockSpec((tm, w), lambda i: (i, 0))
    return pl.pallas_call(
        functools.partial(_proj_gmlp_kernel, gm_width=gm_width, da_width=da_width, q_scale=q_scale),
        grid=(t // tm,),
        in_specs=[row_blk(d), const((d, cols)), const((da_width, d)), const((da_width, d)), const(kfeat.shape),
                  const((1, gm_width)), const((1, gm_width)),
                  const((GM_HEADS, CHUNK, CHUNK)), const((CHUNK, gm_width))],
        out_specs=[row_blk(gm_width), pl.BlockSpec((da_width, tm), lambda i: (0, i)), row_blk(kw),
                   pl.BlockSpec((None, vr, tm), lambda i: (i, 0, 0))],
        out_shape=[jax.ShapeDtypeStruct((t, gm_width), BF16), jax.ShapeDtypeStruct((da_width, t), BF16),
                   jax.ShapeDtypeStruct((t, kw), BF16),
                   jax.ShapeDtypeStruct((t // tm, vr, tm), BF16)],
        compiler_params=pltpu.CompilerParams(dimension_semantics=("arbitrary",),
                                             vmem_limit_bytes=VMEM_LIMIT),
        name="proj_gmlp",
    )(x2, w_in_b, wqt, wvt, kfeat, gm_g, gm_b, w_s, bs_tile)


def _diff_attn_kernel(slope_ref, qt_ref, qaug_ref, k_ref, vt_ref, lq1_ref, lk1_ref, lq2_ref, lk2_ref, gb_ref, o_ref,
                      s_scr, p_scr, mask_scr, acc_scr, *, tq, tk, sq, rc):
    h = pl.program_id(1)
    qi = pl.program_id(2)
    slope2 = slope_ref[h]
    dv = qt_ref.shape[0]
    dk = dv // 2
    n_sq = tq // sq
    n_ch = 2 * n_sq
    n_full = qi

    @pl.when(qi == 0)
    def _():
        kr = lax.broadcasted_iota(jnp.int32, (tk, sq), 0)
        qc = lax.broadcasted_iota(jnp.int32, (tk, sq), 1)
        for si in range(n_sq):
            mask_scr[si] = jnp.where(kr <= qc + si * sq, 0.0, NEG_BIG)

    rowq = lax.broadcasted_iota(jnp.int32, (dv, sq), 0)
    chains = []
    for si in range(n_sq):
        qt = qt_ref[:, si * sq:(si + 1) * sq]
        zero = jnp.zeros_like(qt)
        chains.append((si, jnp.concatenate([jnp.where(rowq < dk, qt, zero), qaug_ref[...]], axis=0)))
        chains.append((si, jnp.concatenate([jnp.where(rowq >= dk, qt, zero), qaug_ref[...]], axis=0)))

    def logits_to_scratch(kj, slot):
        kb = k_ref[pl.ds(pl.multiple_of(kj * tk, tk), tk), :]
        for ci, (_, qm) in enumerate(chains):
            s_scr[slot, ci] = jnp.dot(kb, qm, preferred_element_type=F32)

    def softmax_step(kj, slot, ml, diag):
        out, alphas = [], []
        for ci, (si, _) in enumerate(chains):
            m = ml[ci]
            rel = kj * tk - (qi * tq + si * sq)
            off = rel.astype(F32) * slope2

            def logits(r0):
                t = s_scr[slot, ci, r0:r0 + rc, :]
                return t + mask_scr[si, r0:r0 + rc, :] if diag else t

            cmax = logits(0)
            for r0 in range(rc, tk, rc):
                cmax = jnp.maximum(cmax, logits(r0))
            m_new = jnp.maximum(m, jnp.max(cmax, axis=0, keepdims=True) + off)
            shift = off - m_new
            for r0 in range(0, tk, rc):
                p_scr[slot, ci, r0:r0 + rc, :] = jnp.exp2(logits(r0) + shift).astype(BF16)
            out.append(m_new)
            alphas.append(jnp.exp2(m - m_new))
        return tuple(out), alphas

    def accumulate_previous(kj_prev, slot_prev, alphas):
        vb = vt_ref[kj_prev]
        for ci in range(n_ch):
            pv = jnp.dot(vb, p_scr[slot_prev, ci], preferred_element_type=F32)
            acc_scr[ci] = alphas[ci] * (acc_scr[ci] + pv)

    def step(kj, slot, ml, prefetch_logits, diag):
        if prefetch_logits:
            logits_to_scratch(kj + 1, 1 - slot)
        ml, alphas = softmax_step(kj, slot, ml, diag)
        accumulate_previous(jnp.maximum(kj - 1, 0), 1 - slot, alphas)
        return ml

    def finalize(kj, slot, ml):
        vb = vt_ref[kj]
        lam = (jnp.exp(jnp.sum(lq1_ref[...] * lk1_ref[...], axis=1, keepdims=True))
               - jnp.exp(jnp.sum(lq2_ref[...] * lk2_ref[...], axis=1, keepdims=True)) + LAMBDA_INIT)
        g = jnp.concatenate([gb_ref[...]] * (sq // gb_ref.shape[1]), axis=1)
        for si in range(n_sq):
            a = [acc_scr[2 * si + mp] + jnp.dot(vb, p_scr[slot, 2 * si + mp], preferred_element_type=F32)
                 for mp in range(2)]
            l1, l2 = a[0][dv:dv + 1], a[1][dv:dv + 1]
            att = a[0][:dv] / l1 - lam * (a[1][:dv] / l2)
            ms = jnp.mean(att * att, axis=0, keepdims=True)
            y = att * lax.rsqrt(ms + LN_EPS) * g
            o_ref[si * sq:(si + 1) * sq, :] = (y * (1.0 - LAMBDA_INIT)).T.astype(o_ref.dtype)

    acc_scr[...] = jnp.zeros_like(acc_scr)
    p_scr[1] = jnp.zeros(p_scr.shape[1:], BF16)
    logits_to_scratch(0, 0)
    ml0 = (jnp.full((1, sq), NEG_BIG, F32),) * n_ch

    def pair(i, ml):
        ml = step(2 * i, 0, ml, True, False)
        return step(2 * i + 1, 1, ml, True, False)

    ml = lax.fori_loop(0, n_full // 2, pair, ml0)

    @pl.when(n_full % 2 == 0)
    def _():
        finalize(n_full, 0, step(n_full, 0, ml, False, True))

    @pl.when(n_full % 2 == 1)
    def _():
        ml1 = step(n_full - 1, 0, ml, True, False)
        finalize(n_full, 1, step(n_full, 1, ml1, False, True))


def _diff_attn(qt, qaug, k3, vt3, slopes2, lq1, lk1, lq2, lk2, gb):
    b, s, _ = k3.shape
    dv = qt.shape[0] // DA_HEADS
    kw = k3.shape[2] // DA_HEADS
    dva = vt3.shape[1] // DA_HEADS
    w = dv * DA_HEADS
    tq, tk, sq = min(TQ, s), min(TK, s), min(SQ, s)
    assert tq == tk
    nq, nkb = s // tq, s // tk
    n_ch = 2 * (tq // sq)
    dk = lq1.shape[1]
    vec = pl.BlockSpec((1, dk), lambda bi, hi, qi: (0, 0))
    return pl.pallas_call(
        functools.partial(_diff_attn_kernel, tq=tq, tk=tk, sq=sq, rc=RC_ATTN),
        scratch_shapes=[pltpu.VMEM((2, n_ch, tk, sq), F32), pltpu.VMEM((2, n_ch, tk, sq), BF16),
                        pltpu.VMEM((tq // sq, tk, sq), F32), pltpu.VMEM((n_ch, dva, sq), F32)],
        grid=(b, DA_HEADS, nq),
        in_specs=[pl.BlockSpec(memory_space=pltpu.SMEM),
                  pl.BlockSpec((dv, tq), lambda bi, hi, qi: (hi, bi * nq + qi)),
                  pl.BlockSpec((None, kw - dv, sq), lambda bi, hi, qi: (hi, 0, 0)),
                  pl.BlockSpec((None, s, kw), lambda bi, hi, qi: (bi, 0, hi)),
                  pl.BlockSpec((nkb, dva, tk), lambda bi, hi, qi: (bi, hi, 0)),
                  vec, vec, vec, vec,
                  pl.BlockSpec((None, dv, LANES), lambda bi, hi, qi: (hi, 0, 0))],
        out_specs=pl.BlockSpec((None, tq, dv), lambda bi, hi, qi: (bi, qi, hi)),
        out_shape=jax.ShapeDtypeStruct((b, s, w), BF16),
        compiler_params=pltpu.CompilerParams(dimension_semantics=("arbitrary",) * 3,
                                             vmem_limit_bytes=VMEM_LIMIT),
        name="diff_attn",
    )(slopes2, qt, qaug, k3, vt3, lq1, lk1, lq2, lk2, gb)


def _mix_ln_kernel(ygm_ref, yda_ref, x_ref, wo_ref, g_ref, b_ref, x1_ref, x1b_ref, x1t_ref):
    gw = ygm_ref.shape[1]
    mix = (jnp.dot(ygm_ref[...], wo_ref[:gw, :], preferred_element_type=F32)
           + jnp.dot(yda_ref[...], wo_ref[gw:, :], preferred_element_type=F32))
    x1 = _layer_norm_rows(ALPHA * x_ref[...] + mix, g_ref[...], b_ref[...])
    x1_ref[...] = x1
    x1b_ref[...] = x1.astype(BF16)
    x1t_ref[...] = x1.T.astype(BF16)


def _mix_ln(ygm, yda, x2, wo_b, ln_g, ln_b):
    t, d = x2.shape
    tm = min(TM_PROJ, t)
    gw, dw = ygm.shape[1], yda.shape[1]
    const = lambda shape: pl.BlockSpec(shape, lambda i: (0,) * len(shape))
    row_blk = lambda w: pl.BlockSpec((tm, w), lambda i: (i, 0))
    return pl.pallas_call(
        _mix_ln_kernel,
        grid=(t // tm,),
        in_specs=[row_blk(gw), row_blk(dw), row_blk(d), const(wo_b.shape), const((1, d)), const((1, d))],
        out_specs=[row_blk(d), row_blk(d), pl.BlockSpec((d, tm), lambda i: (0, i))],
        out_shape=[jax.ShapeDtypeStruct((t, d), F32), jax.ShapeDtypeStruct((t, d), BF16),
                   jax.ShapeDtypeStruct((d, t), BF16)],
        compiler_params=pltpu.CompilerParams(dimension_semantics=("arbitrary",),
                                             vmem_limit_bytes=VMEM_LIMIT),
        name="mix_ln",
    )(ygm, yda, x2, wo_b, ln_g, ln_b)


def _top16_exact(val, rank_ref, vals_ref):
    key_iota = lax.broadcasted_iota(jnp.int32, val.shape, 0).astype(F32)
    rank = jnp.full(val.shape, NOT_RANKED, F32)
    for r in range(PEER_TOPK):
        m = jnp.max(val, axis=0, keepdims=True)
        idx = jnp.min(jnp.where(val == m, key_iota, float(N_KEYS)), axis=0, keepdims=True)
        hit = key_iota == idx
        rank = jnp.where(hit, float(r), rank)
        val = jnp.where(hit, -jnp.inf, val)
        vals_ref[r:r + 1, :] = m
    rank_ref[...] = rank


def _top16_if_distinct(val, rank_ref, vals_ref):
    rank = jnp.full(val.shape, NOT_RANKED, F32)
    for r in range(PEER_TOPK):
        m = jnp.max(val, axis=0, keepdims=True)
        hit = val == m
        rank = jnp.where(hit, float(r), rank)
        val = jnp.where(hit, -jnp.inf, val)
        vals_ref[r:r + 1, :] = m
    rank_ref[...] = rank
    return jnp.sum(jnp.where(val == -jnp.inf, 1.0, 0.0), axis=0, keepdims=True)


_PIECES = 10
_J_LIMIT = (8, 8, 8, 5, 4, 3, 2, 2, 2, 8)


def _candidates(va, vb):
    tl = va.shape[1]
    j8 = lax.broadcasted_iota(jnp.int32, (8, tl), 0).astype(F32)
    pieces, flats = [], []
    for p in range(_PIECES):
        if p == 0:
            c, f = vb[0:8] + va[0:1], j8
        elif p == 1:
            c, f = vb[8:16] + va[0:1], j8 + 8.0
        elif p == 9:
            c, f = va[8:16] + vb[0:1], 128.0 + 16.0 * j8
        else:
            i = p - 1
            c, f = vb[0:8] + va[i:i + 1], j8 + 16.0 * i
            c = jnp.where(j8 < float(_J_LIMIT[p]), c, -jnp.inf)
        pieces.append(c)
        flats.append(f)
    return jnp.concatenate(pieces, axis=0), jnp.concatenate(flats, axis=0)


def _joint_exact(cand, flat, s0, cnt_scr, z_scr):
    tl = cand.shape[1]
    i16 = lax.broadcasted_iota(jnp.int32, (PEER_TOPK, tl), 0).astype(F32)
    cnt = jnp.zeros((PEER_TOPK, tl), F32)
    z = jnp.zeros((1, tl), F32)
    for _ in range(PEER_TOPK):
        m = jnp.max(cand, axis=0, keepdims=True)
        idx = jnp.min(jnp.where(cand == m, flat, 1024.0), axis=0, keepdims=True)
        cand = jnp.where(flat == idx, -jnp.inf, cand)
        cnt = cnt + jnp.where(i16 == jnp.floor(idx * (1.0 / PEER_TOPK)), 1.0, 0.0)
        z = z + jnp.exp(m - s0)
    cnt_scr[...] = cnt
    z_scr[0:1, :] = z


def _joint_if_distinct(cand, s0, cnt_scr, z_scr):
    taken = jnp.zeros(cand.shape, F32)
    z = jnp.zeros((1, cand.shape[1]), F32)
    for _ in range(PEER_TOPK):
        m = jnp.max(cand, axis=0, keepdims=True)
        hit = cand == m
        taken = jnp.where(hit, 1.0, taken)
        cand = jnp.where(hit, -jnp.inf, cand)
        z = z + jnp.exp(m - s0)
    rowsum = lambda lo, hi: jnp.sum(taken[lo:hi], axis=0, keepdims=True)
    cnt_scr[0:1, :] = rowsum(0, 16)
    for i in range(1, 8):
        cnt_scr[i:i + 1, :] = rowsum(8 * (i + 1), 8 * (i + 2))
    cnt_scr[8:16, :] = taken[72:80]
    z_scr[0:1, :] = z
    return jnp.sum(taken, axis=0, keepdims=True)


def _key_scores(x1b, wq, ka, kb):
    half = ka.shape[1]
    qa = jnp.dot(x1b, wq[:, :half], preferred_element_type=F32).astype(BF16)
    qb = jnp.dot(x1b, wq[:, half:], preferred_element_type=F32).astype(BF16)
    return _nt_dot(ka, qa), _nt_dot(kb, qb)


def _peer_select_kernel(x1f_ref, wqf_ref, kaf_ref, kbf_ref, x1n_ref, wqn_ref, kan_ref, kbn_ref,
                        cnt_ref, wa_ref, rb_ref, wb_ref,
                        sc_scr, va_ref, vb_ref, ra_scr, rb_scr, cnt_scr, z_scr):
    i = pl.program_id(0)
    h = pl.program_id(1)
    slot = h % 2

    @pl.when((i == 0) & (h == 0))
    def _():
        sa0, sb0 = _key_scores(x1f_ref[...], wqf_ref[...], kaf_ref[...], kbf_ref[...])
        sc_scr[0, 0] = sa0
        sc_scr[0, 1] = sb0

    sa = sc_scr[slot, 0]
    sb = sc_scr[slot, 1]
    sa_next, sb_next = _key_scores(x1n_ref[...], wqn_ref[...], kan_ref[...], kbn_ref[...])
    sc_scr[1 - slot, 0] = sa_next
    sc_scr[1 - slot, 1] = sb_next

    ea = _top16_if_distinct(sa, ra_scr, va_ref)
    eb = _top16_if_distinct(sb, rb_scr, vb_ref)
    va = va_ref[...]
    vb = vb_ref[...]
    cand, _ = _candidates(va, vb)
    ec = _joint_if_distinct(cand, va[0:1] + vb[0:1], cnt_scr, z_scr)
    extracted = jnp.maximum(jnp.maximum(ea, eb), ec)

    @pl.when(jnp.max(extracted) > float(PEER_TOPK))
    def _():
        _top16_exact(sa, ra_scr, va_ref)
        _top16_exact(sb, rb_scr, vb_ref)
        va_x = va_ref[...]
        vb_x = vb_ref[...]
        cand_x, flat_x = _candidates(va_x, vb_x)
        _joint_exact(cand_x, flat_x, va_x[0:1] + vb_x[0:1], cnt_scr, z_scr)

    ra = ra_scr[...].astype(BF16)
    cnt = cnt_scr[...].astype(BF16)
    cnt_key = jnp.zeros(ra.shape, BF16)
    for r in range(PEER_TOPK):
        cnt_key = jnp.where(ra == float(r), cnt[r:r + 1], cnt_key)
    cnt_ref[...] = cnt_key.astype(F32)
    z = z_scr[0:1, :]
    wa_ref[...] = jnp.exp(sa - va_ref[0:1, :]) * (0.5 / z)
    rb_ref[...] = rb_scr[...].astype(BF16)
    wb_ref[...] = jnp.exp(sb - vb_ref[0:1, :]).astype(BF16)


def _peer_select(x1b, wq_b, ka_b, kb_b):
    t, d = x1b.shape
    nh, nk, half = ka_b.shape
    assert nh % 2 == 0
    tl = min(TL_SEL, t)
    nt = t // tl
    qcols = 2 * half
    out_blk = pl.BlockSpec((None, nk, tl), lambda i, h: (h, 0, i))
    out_f32 = jax.ShapeDtypeStruct((nh, nk, t), F32)
    out_b16 = jax.ShapeDtypeStruct((nh, nk, t), BF16)
    first = [pl.BlockSpec((tl, d), lambda i, h: (0, 0)), pl.BlockSpec((d, qcols), lambda i, h: (0, 0)),
             pl.BlockSpec((None, nk, half), lambda i, h: (0, 0, 0)),
             pl.BlockSpec((None, nk, half), lambda i, h: (0, 0, 0))]
    nxt = [pl.BlockSpec((tl, d), lambda i, h: (jnp.minimum(i + (h + 1) // nh, nt - 1), 0)),
           pl.BlockSpec((d, qcols), lambda i, h: (0, (h + 1) % nh)),
           pl.BlockSpec((None, nk, half), lambda i, h: ((h + 1) % nh, 0, 0)),
           pl.BlockSpec((None, nk, half), lambda i, h: ((h + 1) % nh, 0, 0))]
    return pl.pallas_call(
        _peer_select_kernel,
        grid=(nt, nh),
        in_specs=first + nxt,
        out_specs=[out_blk, out_blk, out_blk, out_blk],
        out_shape=[out_f32, out_f32, out_b16, out_b16],
        scratch_shapes=[pltpu.VMEM((2, 2, nk, tl), F32),
                        pltpu.VMEM((PEER_TOPK, tl), F32), pltpu.VMEM((PEER_TOPK, tl), F32),
                        pltpu.VMEM((nk, tl), F32), pltpu.VMEM((nk, tl), F32),
                        pltpu.VMEM((PEER_TOPK, tl), F32), pltpu.VMEM((SUBLANES, tl), F32)],
        compiler_params=pltpu.CompilerParams(dimension_semantics=("arbitrary", "arbitrary"),
                                             vmem_limit_bytes=VMEM_LIMIT),
        name="peer_select",
    )(x1b, wq_b, ka_b, kb_b, x1b, wq_b, ka_b, kb_b)


def _peer_dense_kernel(x1t_ref, u_ref, vt_ref, cnt_ref, wa_ref, rb_ref, wb_ref, o_ref, s_ref, *, sub):
    j = pl.program_id(1)
    eb = u_ref.shape[0]
    tt = x1t_ref.shape[1]
    n_sub = tt // sub
    n_slot = s_ref.shape[0]
    a_rows = eb // N_KEYS
    a_grp = 2
    n_rg = N_KEYS // SUBLANES
    rg_grp = 8
    zero = jnp.zeros((), BF16)

    @pl.when(j == 0)
    def _():
        o_ref[...] = jnp.zeros_like(o_ref)

    def pre_act(c):
        s_ref[c % n_slot] = jnp.dot(u_ref[...], x1t_ref[:, c * sub:(c + 1) * sub], preferred_element_type=F32)

    def gate_and_activate(c):
        ts = slice(c * sub, (c + 1) * sub)
        cnt_b = [cnt_ref[hh, :, ts].astype(BF16) for hh in range(PEER_HEADS)]
        wa_b = [wa_ref[hh, :, ts].astype(BF16) for hh in range(PEER_HEADS)]
        pieces = [None] * (eb // SUBLANES)
        for a0 in range(0, a_rows, a_grp):
            for r0 in range(0, n_rg, rg_grp):
                g = [[jnp.zeros((SUBLANES, sub), BF16) for _ in range(rg_grp)] for _ in range(a_grp)]
                for hh in range(PEER_HEADS):
                    cnt = [jnp.broadcast_to(cnt_b[hh][a0 + k:a0 + k + 1], (SUBLANES, sub)) for k in range(a_grp)]
                    wa = [jnp.broadcast_to(wa_b[hh][a0 + k:a0 + k + 1], (SUBLANES, sub)) for k in range(a_grp)]
                    for r in range(rg_grp):
                        bs = slice((r0 + r) * SUBLANES, (r0 + r + 1) * SUBLANES)
                        rb = rb_ref[hh, bs, ts]
                        wb = wb_ref[hh, bs, ts]
                        for k in range(a_grp):
                            g[k][r] = g[k][r] + jnp.where(rb < cnt[k], wb * wa[k], zero)
                for k in range(a_grp):
                    for r in range(rg_grp):
                        e0 = (a0 + k) * N_KEYS + (r0 + r) * SUBLANES
                        x = s_ref[c % n_slot, e0:e0 + SUBLANES, :]
                        act = x * (1.0 + lax.erf(x * (1.0 / math.sqrt(2.0))))
                        pieces[e0 // SUBLANES] = act.astype(BF16) * g[k][r]
        return jnp.concatenate(pieces, axis=0)

    pre_act(0)
    for c in range(n_sub):
        if c + 1 < n_sub:
            pre_act(c + 1)
        hact = gate_and_activate(c)
        o_ref[:, c * sub:(c + 1) * sub] += jnp.dot(vt_ref[...], hact, preferred_element_type=F32)


def _peer_dense(x1t, u_b, vt_b, cnt, wa, rb, wb):
    d, t = x1t.shape
    ne = u_b.shape[0]
    tt = min(TT_DENSE, t)
    sub = min(SUB_DENSE, tt)
    eb = EB_DENSE
    a_rows = eb // N_KEYS
    row_meta = pl.BlockSpec((PEER_HEADS, a_rows, tt), lambda i, j: (0, j, i))
    key_meta = pl.BlockSpec((PEER_HEADS, N_KEYS, tt), lambda i, j: (0, 0, i))
    return pl.pallas_call(
        functools.partial(_peer_dense_kernel, sub=sub),
        grid=(t // tt, ne // eb),
        in_specs=[pl.BlockSpec((d, tt), lambda i, j: (0, i)),
                  pl.BlockSpec((eb, d), lambda i, j: (j, 0)),
                  pl.BlockSpec((d, eb), lambda i, j: (0, j)),
                  row_meta, row_meta, key_meta, key_meta],
        out_specs=pl.BlockSpec((d, tt), lambda i, j: (0, i)),
        out_shape=jax.ShapeDtypeStruct((d, t), F32),
        scratch_shapes=[pltpu.VMEM((min(2, tt // sub), eb, sub), F32)],
        compiler_params=pltpu.CompilerParams(dimension_semantics=("arbitrary", "arbitrary"),
                                             vmem_limit_bytes=VMEM_LIMIT),
        name="peer_dense",
    )(x1t, u_b, vt_b, cnt, wa, rb, wb)


def _ffn_ln_kernel(ffnt_ref, x1_ref, g_ref, b_ref, o_ref):
    ffn = ffnt_ref[...].T
    o_ref[...] = _layer_norm_rows(ALPHA * x1_ref[...] + ffn, g_ref[...], b_ref[...])


def _ffn_ln(ffnt, x1, ln_g, ln_b):
    t, d = x1.shape
    tm = min(TM_PROJ, t)
    const = lambda shape: pl.BlockSpec(shape, lambda i: (0,) * len(shape))
    return pl.pallas_call(
        _ffn_ln_kernel,
        grid=(t // tm,),
        in_specs=[pl.BlockSpec((d, tm), lambda i: (0, i)), pl.BlockSpec((tm, d), lambda i: (i, 0)),
                  const((1, d)), const((1, d))],
        out_specs=pl.BlockSpec((tm, d), lambda i: (i, 0)),
        out_shape=jax.ShapeDtypeStruct((t, d), F32),
        compiler_params=pltpu.CompilerParams(dimension_semantics=("arbitrary",),
                                             vmem_limit_bytes=VMEM_LIMIT),
        name="ffn_ln",
    )(ffnt, x1, ln_g, ln_b)


def _alibi_features(slopes2, tk, sq):
    assert tk <= 512 and sq <= 256
    c = jnp.asarray(slopes2, F32)
    nh = c.shape[0]
    hi = c.astype(BF16).astype(F32)
    mid = (c - hi).astype(BF16).astype(F32)
    lo = (c - hi - mid).astype(BF16).astype(F32)
    parts = jnp.stack([hi, mid, lo], axis=1)
    kr = np.arange(tk)
    kfeat = jnp.zeros((nh, tk, FEAT_WIDTH), F32)
    kfeat = kfeat.at[:, :, 0:3].set((kr % 256).astype(np.float32)[None, :, None])
    kfeat = kfeat.at[:, :, 3:6].set((256 * (kr // 256)).astype(np.float32)[None, :, None])
    kfeat = kfeat.at[:, :, 6:9].set(jnp.broadcast_to(parts[:, None, :], (nh, tk, 3)))
    qfeat = jnp.zeros((nh, FEAT_WIDTH, sq), F32)
    qfeat = qfeat.at[:, 0:3, :].set(jnp.broadcast_to(parts[:, :, None], (nh, 3, sq)))
    qfeat = qfeat.at[:, 3:6, :].set(jnp.broadcast_to(parts[:, :, None], (nh, 3, sq)))
    qfeat = qfeat.at[:, 6:9, :].set(jnp.broadcast_to(-jnp.arange(sq, dtype=F32)[None, None, :], (nh, 3, sq)))
    return kfeat.astype(BF16), qfeat.astype(BF16)


def kernel(x, w_in, gm_norm_g, gm_norm_b, gm_w_s, gm_b_s, lam_q1, lam_k1, lam_q2, lam_k2, da_norm_g,
           w_o, ln1_g, ln1_b, peer_w_q, peer_keys_a, peer_keys_b, peer_u, peer_v, ln2_g, ln2_b):
    b, s, d = x.shape
    t = b * s
    gm_width = gm_norm_g.shape[0]
    da_width = da_norm_g.shape[0] * da_norm_g.shape[1]
    dk = lam_q1.shape[0]
    assert s % CHUNK == 0 and gm_w_s.shape == (GM_HEADS, CHUNK, CHUNK)
    assert peer_u.shape[0] == N_KEYS * N_KEYS and peer_keys_a.shape[:2] == (PEER_HEADS, N_KEYS)

    row = lambda v: v.reshape(1, -1).astype(F32)
    x2 = x.reshape(t, d)
    bs_tile = jnp.repeat(gm_b_s.T, gm_width // GM_HEADS, axis=1)
    q_scale = (dk ** -0.5) * LOG2E
    assert s % TK == 0
    slopes2 = jnp.asarray(2.0 ** (-(8.0 / DA_HEADS) * np.arange(1, DA_HEADS + 1)) * LOG2E, F32)
    kfeat, qfeat = _alibi_features(slopes2, TK, SQ)
    ygm, qt, k, vt3 = _proj_gmlp(x2, w_in.astype(BF16), kfeat, row(gm_norm_g), row(gm_norm_b), gm_w_s,
                                 bs_tile, gm_width, da_width, q_scale)

    g_lanes = jnp.broadcast_to(da_norm_g.astype(F32)[:, :, None], da_norm_g.shape + (LANES,))
    yda = _diff_attn(qt, qfeat, k.reshape(b, s, -1), vt3, slopes2, row(lam_q1), row(lam_k1), row(lam_q2),
                     row(lam_k2), g_lanes)

    x1, x1b, x1t = _mix_ln(ygm, yda.reshape(t, da_width), x2, w_o.astype(BF16), row(ln1_g), row(ln1_b))
    cnt, wa, rb, wb = _peer_select(x1b, peer_w_q.astype(BF16), peer_keys_a.astype(BF16),
                                   peer_keys_b.astype(BF16))
    ffnt = _peer_dense(x1t, peer_u.astype(BF16), peer_v.astype(BF16).T, cnt, wa, rb, wb)
    out = _ffn_ln(ffnt, x1, row(ln2_g), row(ln2_b))
    return out.reshape(b, s, d)
```

```python
import functools
import math

import jax
import jax.numpy as jnp
import numpy as np
from jax import lax
from jax.experimental import pallas as pl
from jax.experimental.pallas import tpu as pltpu

F32 = jnp.float32
BF16 = jnp.bfloat16

GM_HEADS = 4
CHUNK = 128
DA_HEADS = 4
LAMBDA_INIT = 0.8 - 0.6 * math.exp(-0.3 * (1 - 1))
PEER_HEADS = 8
N_KEYS = 128
PEER_TOPK = 16
DEPTH = 1
ALPHA = (2.0 * DEPTH) ** 0.25
LN_EPS = 1e-5
NEG_BIG = -1e30
LOG2E = 1.4426950408889634
NOT_RANKED = 64.0

LANES = 128
SUBLANES = 8
VMEM_LIMIT = 48 * 1024 * 1024
VMEM_LIMIT_DENSE = 58 * 1024 * 1024
TM_PROJ = 512
TQ = 512
SQ = 256
TK = 512
RC_ATTN = 64
ONES_ROWS = 16
FEAT_WIDTH = 128
TL_SEL = 512
TT_DENSE = 1024
SUB_DENSE = 1024
EB_DENSE = 2048


def _gelu_exact(x):
    return 0.5 * x * (1.0 + lax.erf(x * (1.0 / math.sqrt(2.0))))


def _layer_norm_rows(x, g, b):
    mu = jnp.mean(x, axis=-1, keepdims=True)
    xc = x - mu
    var = jnp.mean(xc * xc, axis=-1, keepdims=True)
    return xc * lax.rsqrt(var + LN_EPS) * g + b


def _nt_dot(a, b):
    return lax.dot_general(a, b, (((1,), (1,)), ((), ())), preferred_element_type=F32)


def _proj_gmlp_kernel(x_ref, w_ref, wqt_ref, wvt_ref, kfeat_ref, g_ref, b_ref, ws_ref, bs_ref,
                      ygm_ref, qt_ref, k_ref, vt_ref, *, gm_width, da_width, q_scale):
    xb = x_ref[...].astype(BF16)
    tm = xb.shape[0]
    hd = gm_width // GM_HEADS
    u = _gelu_exact(jnp.dot(xb, w_ref[:, :gm_width], preferred_element_type=F32))
    z = _gelu_exact(jnp.dot(xb, w_ref[:, gm_width:2 * gm_width], preferred_element_type=F32))
    vn = _layer_norm_rows(z, g_ref[...], b_ref[...]).astype(BF16)
    row = lax.broadcasted_iota(jnp.int32, (CHUNK, CHUNK), 0)
    col = lax.broadcasted_iota(jnp.int32, (CHUNK, CHUNK), 1)
    causal = col <= row
    for hh in range(GM_HEADS):
        w_c = jnp.where(causal, ws_ref[hh], 0.0).astype(BF16)
        for c in range(tm // CHUNK):
            rs = slice(c * CHUNK, (c + 1) * CHUNK)
            cs = slice(hh * hd, (hh + 1) * hd)
            mixed = jnp.dot(w_c, vn[rs, cs], preferred_element_type=F32) + bs_ref[:, cs]
            ygm_ref[rs, cs] = (u[rs, cs] * mixed).astype(BF16)
    o = 2 * gm_width
    dvh = da_width // DA_HEADS
    fw = kfeat_ref.shape[2]
    k = jnp.dot(xb, w_ref[:, o + da_width:o + 2 * da_width], preferred_element_type=F32).astype(BF16)
    qt_ref[...] = (_nt_dot(wqt_ref[...], xb) * q_scale).astype(BF16)
    vt = _nt_dot(wvt_ref[...], xb).astype(BF16)
    ones = jnp.ones((ONES_ROWS, tm), BF16)
    for hh in range(DA_HEADS):
        k_ref[:, hh * (dvh + fw):hh * (dvh + fw) + dvh] = k[:, hh * dvh:(hh + 1) * dvh]
        k_ref[:, hh * (dvh + fw) + dvh:(hh + 1) * (dvh + fw)] = kfeat_ref[hh]
        r0 = hh * (dvh + ONES_ROWS)
        vt_ref[r0:r0 + dvh, :] = vt[hh * dvh:(hh + 1) * dvh]
        vt_ref[r0 + dvh:r0 + dvh + ONES_ROWS, :] = ones


def _proj_gmlp(x2, w_in_b, kfeat, gm_g, gm_b, w_s, bs_tile, gm_width, da_width, q_scale):
    t, d = x2.shape
    tm = TK
    cols = w_in_b.shape[1]
    kw = da_width + DA_HEADS * kfeat.shape[2]
    vr = da_width + DA_HEADS * ONES_ROWS
    o = 2 * gm_width
    wqt = w_in_b[:, o:o + da_width].T
    wvt = w_in_b[:, o + 2 * da_width:o + 3 * da_width].T
    const = lambda shape: pl.BlockSpec(shape, lambda i: (0,) * len(shape))
    row_blk = lambda w: pl.BlockSpec((tm, w), lambda i: (i, 0))
    return pl.pallas_call(
        functools.partial(_proj_gmlp_kernel, gm_width=gm_width, da_width=da_width, q_scale=q_scale),
        grid=(t // tm,),
        in_specs=[row_blk(d), const((d, cols)), const((da_width, d)), const((da_width, d)), const(kfeat.shape),
                  const((1, gm_width)), const((1, gm_width)),
                  const((GM_HEADS, CHUNK, CHUNK)), const((CHUNK, gm_width))],
        out_specs=[row_blk(gm_width), pl.BlockSpec((da_width, tm), lambda i: (0, i)), row_blk(kw),
                   pl.BlockSpec((None, vr, tm), lambda i: (i, 0, 0))],
        out_shape=[jax.ShapeDtypeStruct((t, gm_width), BF16), jax.ShapeDtypeStruct((da_width, t), BF16),
                   jax.ShapeDtypeStruct((t, kw), BF16),
                   jax.ShapeDtypeStruct((t // tm, vr, tm), BF16)],
        compiler_params=pltpu.CompilerParams(dimension_semantics=("arbitrary",),
                                             vmem_limit_bytes=VMEM_LIMIT),
        name="proj_gmlp",
    )(x2, w_in_b, wqt, wvt, kfeat, gm_g, gm_b, w_s, bs_tile)


def _diff_attn_kernel(slope_ref, qt_ref, qaug_ref, k_ref, vt_ref, lq1_ref, lk1_ref, lq2_ref, lk2_ref, gb_ref, o_ref,
                      s_scr, p_scr, mask_scr, acc_scr, *, tq, tk, sq, rc):
    h = pl.program_id(1)
    qi = pl.program_id(2)
    slope2 = slope_ref[h]
    dv = qt_ref.shape[0]
    dk = dv // 2
    n_sq = tq // sq
    n_ch = 2 * n_sq
    n_full = qi

    @pl.when(qi == 0)
    def _():
        kr = lax.broadcasted_iota(jnp.int32, (tk, sq), 0)
        qc = lax.broadcasted_iota(jnp.int32, (tk, sq), 1)
        for si in range(n_sq):
            mask_scr[si] = jnp.where(kr <= qc + si * sq, 0.0, NEG_BIG)

    rowq = lax.broadcasted_iota(jnp.int32, (dv, sq), 0)
    chains = []
    for si in range(n_sq):
        qt = qt_ref[:, si * sq:(si + 1) * sq]
        zero = jnp.zeros_like(qt)
        chains.append((si, jnp.concatenate([jnp.where(rowq < dk, qt, zero), qaug_ref[...]], axis=0)))
        chains.append((si, jnp.concatenate([jnp.where(rowq >= dk, qt, zero), qaug_ref[...]], axis=0)))

    def logits_to_scratch(kj, slot):
        kb = k_ref[pl.ds(pl.multiple_of(kj * tk, tk), tk), :]
        for ci, (_, qm) in enumerate(chains):
            s_scr[slot, ci] = jnp.dot(kb, qm, preferred_element_type=F32)

    def softmax_step(kj, slot, ml, diag):
        out, alphas = [], []
        for ci, (si, _) in enumerate(chains):
            m = ml[ci]
            rel = kj * tk - (qi * tq + si * sq)
            off = rel.astype(F32) * slope2

            def logits(r0):
                t = s_scr[slot, ci, r0:r0 + rc, :]
                return t + mask_scr[si, r0:r0 + rc, :] if diag else t

            cmax = logits(0)
            for r0 in range(rc, tk, rc):
                cmax = jnp.maximum(cmax, logits(r0))
            m_new = jnp.maximum(m, jnp.max(cmax, axis=0, keepdims=True) + off)
            shift = off - m_new
            for r0 in range(0, tk, rc):
                p_scr[slot, ci, r0:r0 + rc, :] = jnp.exp2(logits(r0) + shift).astype(BF16)
            out.append(m_new)
            alphas.append(jnp.exp2(m - m_new))
        return tuple(out), alphas

    def accumulate_previous(kj_prev, slot_prev, alphas):
        vb = vt_ref[kj_prev]
        for ci in range(n_ch):
            pv = jnp.dot(vb, p_scr[slot_prev, ci], preferred_element_type=F32)
            acc_scr[ci] = alphas[ci] * (acc_scr[ci] + pv)

    def step(kj, slot, ml, prefetch_logits, diag):
        if prefetch_logits:
            logits_to_scratch(kj + 1, 1 - slot)
        ml, alphas = softmax_step(kj, slot, ml, diag)
        accumulate_previous(jnp.maximum(kj - 1, 0), 1 - slot, alphas)
        return ml

    def finalize(kj, slot, ml):
        vb = vt_ref[kj]
        lam = (jnp.exp(jnp.sum(lq1_ref[...] * lk1_ref[...], axis=1, keepdims=True))
               - jnp.exp(jnp.sum(lq2_ref[...] * lk2_ref[...], axis=1, keepdims=True)) + LAMBDA_INIT)
        g = jnp.concatenate([gb_ref[...]] * (sq // gb_ref.shape[1]), axis=1)
        for si in range(n_sq):
            a = [acc_scr[2 * si + mp] + jnp.dot(vb, p_scr[slot, 2 * si + mp], preferred_element_type=F32)
                 for mp in range(2)]
            l1, l2 = a[0][dv:dv + 1], a[1][dv:dv + 1]
            att = a[0][:dv] / l1 - lam * (a[1][:dv] / l2)
            ms = jnp.mean(att * att, axis=0, keepdims=True)
            y = att * lax.rsqrt(ms + LN_EPS) * g
            o_ref[si * sq:(si + 1) * sq, :] = (y * (1.0 - LAMBDA_INIT)).T.astype(o_ref.dtype)

    acc_scr[...] = jnp.zeros_like(acc_scr)
    p_scr[1] = jnp.zeros(p_scr.shape[1:], BF16)
    logits_to_scratch(0, 0)
    ml0 = (jnp.full((1, sq), NEG_BIG, F32),) * n_ch

    def pair(i, ml):
        ml = step(2 * i, 0, ml, True, False)
        return step(2 * i + 1, 1, ml, True, False)

    ml = lax.fori_loop(0, n_full // 2, pair, ml0)

    @pl.when(n_full % 2 == 0)
    def _():
        finalize(n_full, 0, step(n_full, 0, ml, False, True))

    @pl.when(n_full % 2 == 1)
    def _():
        ml1 = step(n_full - 1, 0, ml, True, False)
        finalize(n_full, 1, step(n_full, 1, ml1, False, True))


def _diff_attn(qt, qaug, k3, vt3, slopes2, lq1, lk1, lq2, lk2, gb):
    b, s, _ = k3.shape
    dv = qt.shape[0] // DA_HEADS
    kw = k3.shape[2] // DA_HEADS
    dva = vt3.shape[1] // DA_HEADS
    w = dv * DA_HEADS
    tq, tk, sq = min(TQ, s), min(TK, s), min(SQ, s)
    assert tq == tk
    nq, nkb = s // tq, s // tk
    n_ch = 2 * (tq // sq)
    dk = lq1.shape[1]
    vec = pl.BlockSpec((1, dk), lambda bi, hi, qi: (0, 0))
    return pl.pallas_call(
        functools.partial(_diff_attn_kernel, tq=tq, tk=tk, sq=sq, rc=RC_ATTN),
        scratch_shapes=[pltpu.VMEM((2, n_ch, tk, sq), F32), pltpu.VMEM((2, n_ch, tk, sq), BF16),
                        pltpu.VMEM((tq // sq, tk, sq), F32), pltpu.VMEM((n_ch, dva, sq), F32)],
        grid=(b, DA_HEADS, nq),
        in_specs=[pl.BlockSpec(memory_space=pltpu.SMEM),
                  pl.BlockSpec((dv, tq), lambda bi, hi, qi: (hi, bi * nq + qi)),
                  pl.BlockSpec((None, kw - dv, sq), lambda bi, hi, qi: (hi, 0, 0)),
                  pl.BlockSpec((None, s, kw), lambda bi, hi, qi: (bi, 0, hi)),
                  pl.BlockSpec((nkb, dva, tk), lambda bi, hi, qi: (bi, hi, 0)),
                  vec, vec, vec, vec,
                  pl.BlockSpec((None, dv, LANES), lambda bi, hi, qi: (hi, 0, 0))],
        out_specs=pl.BlockSpec((None, tq, dv), lambda bi, hi, qi: (bi, qi, hi)),
        out_shape=jax.ShapeDtypeStruct((b, s, w), BF16),
        compiler_params=pltpu.CompilerParams(dimension_semantics=("arbitrary",) * 3,
                                             vmem_limit_bytes=VMEM_LIMIT),
        name="diff_attn",
    )(slopes2, qt, qaug, k3, vt3, lq1, lk1, lq2, lk2, gb)


def _mix_ln_kernel(ygm_ref, yda_ref, x_ref, wo_ref, g_ref, b_ref, x1_ref, x1b_ref, x1t_ref):
    gw = ygm_ref.shape[1]
    mix = (jnp.dot(ygm_ref[...], wo_ref[:gw, :], preferred_element_type=F32)
           + jnp.dot(yda_ref[...], wo_ref[gw:, :], preferred_element_type=F32))
    x1 = _layer_norm_rows(ALPHA * x_ref[...] + mix, g_ref[...], b_ref[...])
    x1_ref[...] = x1
    x1b_ref[...] = x1.astype(BF16)
    x1t_ref[...] = x1.T.astype(BF16)


def _mix_ln(ygm, yda, x2, wo_b, ln_g, ln_b):
    t, d = x2.shape
    tm = min(TM_PROJ, t)
    gw, dw = ygm.shape[1], yda.shape[1]
    const = lambda shape: pl.BlockSpec(shape, lambda i: (0,) * len(shape))
    row_blk = lambda w: pl.BlockSpec((tm, w), lambda i: (i, 0))
    return pl.pallas_call(
        _mix_ln_kernel,
        grid=(t // tm,),
        in_specs=[row_blk(gw), row_blk(dw), row_blk(d), const(wo_b.shape), const((1, d)), const((1, d))],
        out_specs=[row_blk(d), row_blk(d), pl.BlockSpec((d, tm), lambda i: (0, i))],
        out_shape=[jax.ShapeDtypeStruct((t, d), F32), jax.ShapeDtypeStruct((t, d), BF16),
                   jax.ShapeDtypeStruct((d, t), BF16)],
        compiler_params=pltpu.CompilerParams(dimension_semantics=("arbitrary",),
                                             vmem_limit_bytes=VMEM_LIMIT),
        name="mix_ln",
    )(ygm, yda, x2, wo_b, ln_g, ln_b)


def _top16_exact(val, rank_ref, vals_ref):
    key_iota = lax.broadcasted_iota(jnp.int32, val.shape, 0).astype(F32)
    rank = jnp.full(val.shape, NOT_RANKED, F32)
    for r in range(PEER_TOPK):
        m = jnp.max(val, axis=0, keepdims=True)
        idx = jnp.min(jnp.where(val == m, key_iota, float(N_KEYS)), axis=0, keepdims=True)
        hit = key_iota == idx
        rank = jnp.where(hit, float(r), rank)
        val = jnp.where(hit, -jnp.inf, val)
        vals_ref[r:r + 1, :] = m
    rank_ref[...] = rank


def _top16_if_distinct(val, rank_ref, vals_ref):
    rank = jnp.full(val.shape, NOT_RANKED, F32)
    for r in range(PEER_TOPK):
        m = jnp.max(val, axis=0, keepdims=True)
        hit = val == m
        rank = jnp.where(hit, float(r), rank)
        val = jnp.where(hit, -jnp.inf, val)
        vals_ref[r:r + 1, :] = m
    rank_ref[...] = rank
    return jnp.sum(jnp.where(val == -jnp.inf, 1.0, 0.0), axis=0, keepdims=True)


_PIECES = 10
_J_LIMIT = (8, 8, 8, 5, 4, 3, 2, 2, 2, 8)


def _candidates(va, vb):
    tl = va.shape[1]
    j8 = lax.broadcasted_iota(jnp.int32, (8, tl), 0).astype(F32)
    pieces, flats = [], []
    for p in range(_PIECES):
        if p == 0:
            c, f = vb[0:8] + va[0:1], j8
        elif p == 1:
            c, f = vb[8:16] + va[0:1], j8 + 8.0
        elif p == 9:
            c, f = va[8:16] + vb[0:1], 128.0 + 16.0 * j8
        else:
            i = p - 1
            c, f = vb[0:8] + va[i:i + 1], j8 + 16.0 * i
            c = jnp.where(j8 < float(_J_LIMIT[p]), c, -jnp.inf)
        pieces.append(c)
        flats.append(f)
    return jnp.concatenate(pieces, axis=0), jnp.concatenate(flats, axis=0)


def _joint_exact(cand, flat, s0, cnt_scr, z_scr):
    tl = cand.shape[1]
    i16 = lax.broadcasted_iota(jnp.int32, (PEER_TOPK, tl), 0).astype(F32)
    cnt = jnp.zeros((PEER_TOPK, tl), F32)
    z = jnp.zeros((1, tl), F32)
    for _ in range(PEER_TOPK):
        m = jnp.max(cand, axis=0, keepdims=True)
        idx = jnp.min(jnp.where(cand == m, flat, 1024.0), axis=0, keepdims=True)
        cand = jnp.where(flat == idx, -jnp.inf, cand)
        cnt = cnt + jnp.where(i16 == jnp.floor(idx * (1.0 / PEER_TOPK)), 1.0, 0.0)
        z = z + jnp.exp(m - s0)
    cnt_scr[...] = cnt
    z_scr[0:1, :] = z


def _joint_if_distinct(cand, s0, cnt_scr, z_scr):
    taken = jnp.zeros(cand.shape, F32)
    z = jnp.zeros((1, cand.shape[1]), F32)
    for _ in range(PEER_TOPK):
        m = jnp.max(cand, axis=0, keepdims=True)
        hit = cand == m
        taken = jnp.where(hit, 1.0, taken)
        cand = jnp.where(hit, -jnp.inf, cand)
        z = z + jnp.exp(m - s0)
    rowsum = lambda lo, hi: jnp.sum(taken[lo:hi], axis=0, keepdims=True)
    cnt_scr[0:1, :] = rowsum(0, 16)
    for i in range(1, 8):
        cnt_scr[i:i + 1, :] = rowsum(8 * (i + 1), 8 * (i + 2))
    cnt_scr[8:16, :] = taken[72:80]
    z_scr[0:1, :] = z
    return jnp.sum(taken, axis=0, keepdims=True)


def _key_scores(x1b, wq, ka, kb):
    half = ka.shape[1]
    qa = jnp.dot(x1b, wq[:, :half], preferred_element_type=F32).astype(BF16)
    qb = jnp.dot(x1b, wq[:, half:], preferred_element_type=F32).astype(BF16)
    return _nt_dot(ka, qa), _nt_dot(kb, qb)


def _peer_select_kernel(x1f_ref, wqf_ref, kaf_ref, kbf_ref, x1n_ref, wqn_ref, kan_ref, kbn_ref,
                        cnt_ref, wa_ref, rb_ref, wb_ref,
                        sc_scr, va_ref, vb_ref, ra_scr, rb_scr, cnt_scr, z_scr):
    i = pl.program_id(0)
    h = pl.program_id(1)
    slot = h % 2

    @pl.when((i == 0) & (h == 0))
    def _():
        sa0, sb0 = _key_scores(x1f_ref[...], wqf_ref[...], kaf_ref[...], kbf_ref[...])
        sc_scr[0, 0] = sa0
        sc_scr[0, 1] = sb0

    sa = sc_scr[slot, 0]
    sb = sc_scr[slot, 1]
    sa_next, sb_next = _key_scores(x1n_ref[...], wqn_ref[...], kan_ref[...], kbn_ref[...])
    sc_scr[1 - slot, 0] = sa_next
    sc_scr[1 - slot, 1] = sb_next

    ea = _top16_if_distinct(sa, ra_scr, va_ref)
    eb = _top16_if_distinct(sb, rb_scr, vb_ref)
    va = va_ref[...]
    vb = vb_ref[...]
    cand, _ = _candidates(va, vb)
    ec = _joint_if_distinct(cand, va[0:1] + vb[0:1], cnt_scr, z_scr)
    extracted = jnp.maximum(jnp.maximum(ea, eb), ec)

    @pl.when(jnp.max(extracted) > float(PEER_TOPK))
    def _():
        _top16_exact(sa, ra_scr, va_ref)
        _top16_exact(sb, rb_scr, vb_ref)
        va_x = va_ref[...]
        vb_x = vb_ref[...]
        cand_x, flat_x = _candidates(va_x, vb_x)
        _joint_exact(cand_x, flat_x, va_x[0:1] + vb_x[0:1], cnt_scr, z_scr)

    ra = ra_scr[...].astype(BF16)
    cnt = cnt_scr[...].astype(BF16)
    cnt_key = jnp.zeros(ra.shape, BF16)
    for r in range(PEER_TOPK):
        cnt_key = jnp.where(ra == float(r), cnt[r:r + 1], cnt_key)
    cnt_ref[...] = cnt_key.astype(F32)
    z = z_scr[0:1, :]
    wa_ref[...] = jnp.exp(sa - va_ref[0:1, :]) * (0.5 / z)
    rb_ref[...] = rb_scr[...].astype(BF16)
    wb_ref[...] = jnp.exp(sb - vb_ref[0:1, :]).astype(BF16)


def _peer_select(x1b, wq_b, ka_b, kb_b):
    t, d = x1b.shape
    nh, nk, half = ka_b.shape
    assert nh % 2 == 0
    tl = min(TL_SEL, t)
    nt = t // tl
    qcols = 2 * half
    out_blk = pl.BlockSpec((None, nk, tl), lambda i, h: (h, 0, i))
    out_f32 = jax.ShapeDtypeStruct((nh, nk, t), F32)
    out_b16 = jax.ShapeDtypeStruct((nh, nk, t), BF16)
    first = [pl.BlockSpec((tl, d), lambda i, h: (0, 0)), pl.BlockSpec((d, qcols), lambda i, h: (0, 0)),
             pl.BlockSpec((None, nk, half), lambda i, h: (0, 0, 0)),
             pl.BlockSpec((None, nk, half), lambda i, h: (0, 0, 0))]
    nxt = [pl.BlockSpec((tl, d), lambda i, h: (jnp.minimum(i + (h + 1) // nh, nt - 1), 0)),
           pl.BlockSpec((d, qcols), lambda i, h: (0, (h + 1) % nh)),
           pl.BlockSpec((None, nk, half), lambda i, h: ((h + 1) % nh, 0, 0)),
           pl.BlockSpec((None, nk, half), lambda i, h: ((h + 1) % nh, 0, 0))]
    return pl.pallas_call(
        _peer_select_kernel,
        grid=(nt, nh),
        in_specs=first + nxt,
        out_specs=[out_blk, out_blk, out_blk, out_blk],
        out_shape=[out_f32, out_f32, out_b16, out_b16],
        scratch_shapes=[pltpu.VMEM((2, 2, nk, tl), F32),
                        pltpu.VMEM((PEER_TOPK, tl), F32), pltpu.VMEM((PEER_TOPK, tl), F32),
                        pltpu.VMEM((nk, tl), F32), pltpu.VMEM((nk, tl), F32),
                        pltpu.VMEM((PEER_TOPK, tl), F32), pltpu.VMEM((SUBLANES, tl), F32)],
        compiler_params=pltpu.CompilerParams(dimension_semantics=("arbitrary", "arbitrary"),
                                             vmem_limit_bytes=VMEM_LIMIT),
        name="peer_select",
    )(x1b, wq_b, ka_b, kb_b, x1b, wq_b, ka_b, kb_b)


def _peer_dense_kernel(x1t_ref, u_ref, vt_ref, cnt_ref, wa_ref, rb_ref, wb_ref, o_ref, s_ref, *, sub):
    j = pl.program_id(1)
    eb = u_ref.shape[0]
    tt = x1t_ref.shape[1]
    n_sub = tt // sub
    n_slot = s_ref.shape[0]
    a_rows = eb // N_KEYS
    a_grp = 2
    n_rg = N_KEYS // SUBLANES
    rg_grp = 8
    zero = jnp.zeros((), BF16)

    @pl.when(j == 0)
    def _():
        o_ref[...] = jnp.zeros_like(o_ref)

    def pre_act(c):
        s_ref[c % n_slot] = jnp.dot(u_ref[...], x1t_ref[:, c * sub:(c + 1) * sub], preferred_element_type=F32)

    def gate_and_activate(c):
        ts = slice(c * sub, (c + 1) * sub)
        cnt_b = [cnt_ref[hh, :, ts].astype(BF16) for hh in range(PEER_HEADS)]
        wa_b = [wa_ref[hh, :, ts].astype(BF16) for hh in range(PEER_HEADS)]
        pieces = [None] * (eb // SUBLANES)
        for a0 in range(0, a_rows, a_grp):
            for r0 in range(0, n_rg, rg_grp):
                g = [[jnp.zeros((SUBLANES, sub), BF16) for _ in range(rg_grp)] for _ in range(a_grp)]
                for hh in range(PEER_HEADS):
                    cnt = [jnp.broadcast_to(cnt_b[hh][a0 + k:a0 + k + 1], (SUBLANES, sub)) for k in range(a_grp)]
                    wa = [jnp.broadcast_to(wa_b[hh][a0 + k:a0 + k + 1], (SUBLANES, sub)) for k in range(a_grp)]
                    for r in range(rg_grp):
                        bs = slice((r0 + r) * SUBLANES, (r0 + r + 1) * SUBLANES)
                        rb = rb_ref[hh, bs, ts]
                        wb = wb_ref[hh, bs, ts]
                        for k in range(a_grp):
                            g[k][r] = g[k][r] + jnp.where(rb < cnt[k], wb * wa[k], zero)
                for k in range(a_grp):
                    for r in range(rg_grp):
                        e0 = (a0 + k) * N_KEYS + (r0 + r) * SUBLANES
                        x = s_ref[c % n_slot, e0:e0 + SUBLANES, :]
                        act = x * (1.0 + lax.erf(x * (1.0 / math.sqrt(2.0))))
                        pieces[e0 // SUBLANES] = act.astype(BF16) * g[k][r]
        return jnp.concatenate(pieces, axis=0)

    pre_act(0)
    for c in range(n_sub):
        if c + 1 < n_sub:
            pre_act(c + 1)
        hact = gate_and_activate(c)
        o_ref[:, c * sub:(c + 1) * sub] += jnp.dot(vt_ref[...], hact, preferred_element_type=F32)


def _peer_dense(x1t, u_b, vt_b, cnt, wa, rb, wb):
    d, t = x1t.shape
    ne = u_b.shape[0]
    tt = min(TT_DENSE, t)
    sub = min(SUB_DENSE, tt)
    eb = EB_DENSE
    a_rows = eb // N_KEYS
    row_meta = pl.BlockSpec((PEER_HEADS, a_rows, tt), lambda i, j: (0, j, i))
    key_meta = pl.BlockSpec((PEER_HEADS, N_KEYS, tt), lambda i, j: (0, 0, i))
    return pl.pallas_call(
        functools.partial(_peer_dense_kernel, sub=sub),
        grid=(t // tt, ne // eb),
        in_specs=[pl.BlockSpec((d, tt), lambda i, j: (0, i)),
                  pl.BlockSpec((eb, d), lambda i, j: (j, 0)),
                  pl.BlockSpec((d, eb), lambda i, j: (0, j)),
                  row_meta, row_meta, key_meta, key_meta],
        out_specs=pl.BlockSpec((d, tt), lambda i, j: (0, i)),
        out_shape=jax.ShapeDtypeStruct((d, t), F32),
        scratch_shapes=[pltpu.VMEM((min(2, tt // sub), eb, sub), F32)],
        compiler_params=pltpu.CompilerParams(dimension_semantics=("arbitrary", "arbitrary"),
                                             vmem_limit_bytes=VMEM_LIMIT_DENSE),
        name="peer_dense",
    )(x1t, u_b, vt_b, cnt, wa, rb, wb)


def _ffn_ln_kernel(ffnt_ref, x1_ref, g_ref, b_ref, o_ref):
    ffn = ffnt_ref[...].T
    o_ref[...] = _layer_norm_rows(ALPHA * x1_ref[...] + ffn, g_ref[...], b_ref[...])


def _ffn_ln(ffnt, x1, ln_g, ln_b):
    t, d = x1.shape
    tm = min(TM_PROJ, t)
    const = lambda shape: pl.BlockSpec(shape, lambda i: (0,) * len(shape))
    return pl.pallas_call(
        _ffn_ln_kernel,
        grid=(t // tm,),
        in_specs=[pl.BlockSpec((d, tm), lambda i: (0, i)), pl.BlockSpec((tm, d), lambda i: (i, 0)),
                  const((1, d)), const((1, d))],
        out_specs=pl.BlockSpec((tm, d), lambda i: (i, 0)),
        out_shape=jax.ShapeDtypeStruct((t, d), F32),
        compiler_params=pltpu.CompilerParams(dimension_semantics=("arbitrary",),
                                             vmem_limit_bytes=VMEM_LIMIT),
        name="ffn_ln",
    )(ffnt, x1, ln_g, ln_b)


def _alibi_features(slopes2, tk, sq):
    assert tk <= 512 and sq <= 256
    c = jnp.asarray(slopes2, F32)
    nh = c.shape[0]
    hi = c.astype(BF16).astype(F32)
    mid = (c - hi).astype(BF16).astype(F32)
    lo = (c - hi - mid).astype(BF16).astype(F32)
    parts = jnp.stack([hi, mid, lo], axis=1)
    kr = np.arange(tk)
    kfeat = jnp.zeros((nh, tk, FEAT_WIDTH), F32)
    kfeat = kfeat.at[:, :, 0:3].set((kr % 256).astype(np.float32)[None, :, None])
    kfeat = kfeat.at[:, :, 3:6].set((256 * (kr // 256)).astype(np.float32)[None, :, None])
    kfeat = kfeat.at[:, :, 6:9].set(jnp.broadcast_to(parts[:, None, :], (nh, tk, 3)))
    qfeat = jnp.zeros((nh, FEAT_WIDTH, sq), F32)
    qfeat = qfeat.at[:, 0:3, :].set(jnp.broadcast_to(parts[:, :, None], (nh, 3, sq)))
    qfeat = qfeat.at[:, 3:6, :].set(jnp.broadcast_to(parts[:, :, None], (nh, 3, sq)))
    qfeat = qfeat.at[:, 6:9, :].set(jnp.broadcast_to(-jnp.arange(sq, dtype=F32)[None, None, :], (nh, 3, sq)))
    return kfeat.astype(BF16), qfeat.astype(BF16)


def kernel(x, w_in, gm_norm_g, gm_norm_b, gm_w_s, gm_b_s, lam_q1, lam_k1, lam_q2, lam_k2, da_norm_g,
           w_o, ln1_g, ln1_b, peer_w_q, peer_keys_a, peer_keys_b, peer_u, peer_v, ln2_g, ln2_b):
    b, s, d = x.shape
    t = b * s
    gm_width = gm_norm_g.shape[0]
    da_width = da_norm_g.shape[0] * da_norm_g.shape[1]
    dk = lam_q1.shape[0]
    assert s % CHUNK == 0 and gm_w_s.shape == (GM_HEADS, CHUNK, CHUNK)
    assert peer_u.shape[0] == N_KEYS * N_KEYS and peer_keys_a.shape[:2] == (PEER_HEADS, N_KEYS)

    row = lambda v: v.reshape(1, -1).astype(F32)
    x2 = x.reshape(t, d)
    bs_tile = jnp.repeat(gm_b_s.T, gm_width // GM_HEADS, axis=1)
    q_scale = (dk ** -0.5) * LOG2E
    assert s % TK == 0
    slopes2 = jnp.asarray(2.0 ** (-(8.0 / DA_HEADS) * np.arange(1, DA_HEADS + 1)) * LOG2E, F32)
    kfeat, qfeat = _alibi_features(slopes2, TK, SQ)
    ygm, qt, k, vt3 = _proj_gmlp(x2, w_in.astype(BF16), kfeat, row(gm_norm_g), row(gm_norm_b), gm_w_s,
                                 bs_tile, gm_width, da_width, q_scale)

    g_lanes = jnp.broadcast_to(da_norm_g.astype(F32)[:, :, None], da_norm_g.shape + (LANES,))
    yda = _diff_attn(qt, qfeat, k.reshape(b, s, -1), vt3, slopes2, row(lam_q1), row(lam_k1), row(lam_q2),
                     row(lam_k2), g_lanes)

    x1, x1b, x1t = _mix_ln(ygm, yda.reshape(t, da_width), x2, w_o.astype(BF16), row(ln1_g), row(ln1_b))
    cnt, wa, rb, wb = _peer_select(x1b, peer_w_q.astype(BF16), peer_keys_a.astype(BF16),
                                   peer_keys_b.astype(BF16))
    ffnt = _peer_dense(x1t, peer_u.astype(BF16), peer_v.astype(BF16).T, cnt, wa, rb, wb)
    out = _ffn_ln(ffnt, x1, row(ln2_g), row(ln2_b))
    return out.reshape(b, s, d)
```

```python
import functools
import math

import jax
import jax.numpy as jnp
import numpy as np
from jax import lax
from jax.experimental import pallas as pl
from jax.experimental.pallas import tpu as pltpu

F32 = jnp.float32
BF16 = jnp.bfloat16

GM_HEADS = 4
CHUNK = 128
DA_HEADS = 4
LAMBDA_INIT = 0.8 - 0.6 * math.exp(-0.3 * (1 - 1))
PEER_HEADS = 8
N_KEYS = 128
PEER_TOPK = 16
DEPTH = 1
ALPHA = (2.0 * DEPTH) ** 0.25
LN_EPS = 1e-5
NEG_BIG = -1e30
LOG2E = 1.4426950408889634
NOT_RANKED = 64.0

LANES = 128
SUBLANES = 8
VMEM_LIMIT = 48 * 1024 * 1024
VMEM_LIMIT_DENSE = 58 * 1024 * 1024
TM_PROJ = 1024
TQ = 512
SQ = 256
TK = 512
RC_ATTN = 64
ONES_ROWS = 16
FEAT_WIDTH = 128
TL_SEL = 512
TT_DENSE = 1024
SUB_DENSE = 1024
EB_DENSE = 2048


def _gelu_exact(x):
    return 0.5 * x * (1.0 + lax.erf(x * (1.0 / math.sqrt(2.0))))


def _layer_norm_rows(x, g, b):
    mu = jnp.mean(x, axis=-1, keepdims=True)
    xc = x - mu
    var = jnp.mean(xc * xc, axis=-1, keepdims=True)
    return xc * lax.rsqrt(var + LN_EPS) * g + b


def _nt_dot(a, b):
    return lax.dot_general(a, b, (((1,), (1,)), ((), ())), preferred_element_type=F32)


def _proj_gmlp_kernel(x_ref, w_ref, wqt_ref, wvt_ref, kfeat_ref, g_ref, b_ref, ws_ref, bs_ref,
                      ygm_ref, qt_ref, k_ref, vt_ref, *, gm_width, da_width, q_scale):
    xb = x_ref[...].astype(BF16)
    tm = xb.shape[0]
    hd = gm_width // GM_HEADS
    u = _gelu_exact(jnp.dot(xb, w_ref[:, :gm_width], preferred_element_type=F32))
    z = _gelu_exact(jnp.dot(xb, w_ref[:, gm_width:2 * gm_width], preferred_element_type=F32))
    vn = _layer_norm_rows(z, g_ref[...], b_ref[...]).astype(BF16)
    row = lax.broadcasted_iota(jnp.int32, (CHUNK, CHUNK), 0)
    col = lax.broadcasted_iota(jnp.int32, (CHUNK, CHUNK), 1)
    causal = col <= row
    for hh in range(GM_HEADS):
        w_c = jnp.where(causal, ws_ref[hh], 0.0).astype(BF16)
        for c in range(tm // CHUNK):
            rs = slice(c * CHUNK, (c + 1) * CHUNK)
            cs = slice(hh * hd, (hh + 1) * hd)
            mixed = jnp.dot(w_c, vn[rs, cs], preferred_element_type=F32) + bs_ref[:, cs]
            ygm_ref[rs, cs] = (u[rs, cs] * mixed).astype(BF16)
    o = 2 * gm_width
    dvh = da_width // DA_HEADS
    fw = kfeat_ref.shape[2]
    k = jnp.dot(xb, w_ref[:, o + da_width:o + 2 * da_width], preferred_element_type=F32).astype(BF16)
    qt_ref[...] = (_nt_dot(wqt_ref[...], xb) * q_scale).astype(BF16)
    vt = _nt_dot(wvt_ref[...], xb).astype(BF16)
    ones = jnp.ones((ONES_ROWS, tm), BF16)
    for hh in range(DA_HEADS):
        k_ref[:, hh * (dvh + fw):hh * (dvh + fw) + dvh] = k[:, hh * dvh:(hh + 1) * dvh]
        k_ref[:, hh * (dvh + fw) + dvh:(hh + 1) * (dvh + fw)] = kfeat_ref[hh]
        r0 = hh * (dvh + ONES_ROWS)
        vt_ref[r0:r0 + dvh, :] = vt[hh * dvh:(hh + 1) * dvh]
        vt_ref[r0 + dvh:r0 + dvh + ONES_ROWS, :] = ones


def _proj_gmlp(x2, w_in_b, kfeat, gm_g, gm_b, w_s, bs_tile, gm_width, da_width, q_scale):
    t, d = x2.shape
    tm = TK
    cols = w_in_b.shape[1]
    kw = da_width + DA_HEADS * kfeat.shape[2]
    vr = da_width + DA_HEADS * ONES_ROWS
    o = 2 * gm_width
    wqt = w_in_b[:, o:o + da_width].T
    wvt = w_in_b[:, o + 2 * da_width:o + 3 * da_width].T
    const = lambda shape: pl.BlockSpec(shape, lambda i: (0,) * len(shape))
    row_blk = lambda w: pl.BlockSpec((tm, w), lambda i: (i, 0))
    return pl.pallas_call(
        functools.partial(_proj_gmlp_kernel, gm_width=gm_width, da_width=da_width, q_scale=q_scale),
        grid=(t // tm,),
        in_specs=[row_blk(d), const((d, cols)), const((da_width, d)), const((da_width, d)), const(kfeat.shape),
                  const((1, gm_width)), const((1, gm_width)),
                  const((GM_HEADS, CHUNK, CHUNK)), const((CHUNK, gm_width))],
        out_specs=[row_blk(gm_width), pl.BlockSpec((da_width, tm), lambda i: (0, i)), row_blk(kw),
                   pl.BlockSpec((None, vr, tm), lambda i: (i, 0, 0))],
        out_shape=[jax.ShapeDtypeStruct((t, gm_width), BF16), jax.ShapeDtypeStruct((da_width, t), BF16),
                   jax.ShapeDtypeStruct((t, kw), BF16),
                   jax.ShapeDtypeStruct((t // tm, vr, tm), BF16)],
        compiler_params=pltpu.CompilerParams(dimension_semantics=("arbitrary",),
                                             vmem_limit_bytes=VMEM_LIMIT),
        name="proj_gmlp",
    )(x2, w_in_b, wqt, wvt, kfeat, gm_g, gm_b, w_s, bs_tile)


def _diff_attn_kernel(slope_ref, qt_ref, qaug_ref, k_ref, vt_ref, lq1_ref, lk1_ref, lq2_ref, lk2_ref, gb_ref, o_ref,
                      s_scr, p_scr, mask_scr, acc_scr, *, tq, tk, sq, rc):
    h = pl.program_id(1)
    qi = pl.program_id(2)
    slope2 = slope_ref[h]
    dv = qt_ref.shape[0]
    dk = dv // 2
    n_sq = tq // sq
    n_ch = 2 * n_sq
    n_full = qi

    @pl.when(qi == 0)
    def _():
        kr = lax.broadcasted_iota(jnp.int32, (tk, sq), 0)
        qc = lax.broadcasted_iota(jnp.int32, (tk, sq), 1)
        for si in range(n_sq):
            mask_scr[si] = jnp.where(kr <= qc + si * sq, 0.0, NEG_BIG)

    rowq = lax.broadcasted_iota(jnp.int32, (dv, sq), 0)
    chains = []
    for si in range(n_sq):
        qt = qt_ref[:, si * sq:(si + 1) * sq]
        zero = jnp.zeros_like(qt)
        chains.append((si, jnp.concatenate([jnp.where(rowq < dk, qt, zero), qaug_ref[...]], axis=0)))
        chains.append((si, jnp.concatenate([jnp.where(rowq >= dk, qt, zero), qaug_ref[...]], axis=0)))

    def logits_to_scratch(kj, slot):
        kb = k_ref[pl.ds(pl.multiple_of(kj * tk, tk), tk), :]
        for ci, (_, qm) in enumerate(chains):
            s_scr[slot, ci] = jnp.dot(kb, qm, preferred_element_type=F32)

    def softmax_step(kj, slot, ml, diag):
        out, alphas = [], []
        for ci, (si, _) in enumerate(chains):
            m = ml[ci]
            rel = kj * tk - (qi * tq + si * sq)
            off = rel.astype(F32) * slope2

            def logits(r0):
                t = s_scr[slot, ci, r0:r0 + rc, :]
                return t + mask_scr[si, r0:r0 + rc, :] if diag else t

            cmax = logits(0)
            for r0 in range(rc, tk, rc):
                cmax = jnp.maximum(cmax, logits(r0))
            m_new = jnp.maximum(m, jnp.max(cmax, axis=0, keepdims=True) + off)
            shift = off - m_new
            for r0 in range(0, tk, rc):
                p_scr[slot, ci, r0:r0 + rc, :] = jnp.exp2(logits(r0) + shift).astype(BF16)
            out.append(m_new)
            alphas.append(jnp.exp2(m - m_new))
        return tuple(out), alphas

    def accumulate_previous(kj_prev, slot_prev, alphas):
        vb = vt_ref[kj_prev]
        for ci in range(n_ch):
            pv = jnp.dot(vb, p_scr[slot_prev, ci], preferred_element_type=F32)
            acc_scr[ci] = alphas[ci] * (acc_scr[ci] + pv)

    def step(kj, slot, ml, prefetch_logits, diag):
        if prefetch_logits:
            logits_to_scratch(kj + 1, 1 - slot)
        ml, alphas = softmax_step(kj, slot, ml, diag)
        accumulate_previous(jnp.maximum(kj - 1, 0), 1 - slot, alphas)
        return ml

    def finalize(kj, slot, ml):
        vb = vt_ref[kj]
        lam = (jnp.exp(jnp.sum(lq1_ref[...] * lk1_ref[...], axis=1, keepdims=True))
               - jnp.exp(jnp.sum(lq2_ref[...] * lk2_ref[...], axis=1, keepdims=True)) + LAMBDA_INIT)
        g = jnp.concatenate([gb_ref[...]] * (sq // gb_ref.shape[1]), axis=1)
        for si in range(n_sq):
            a = [acc_scr[2 * si + mp] + jnp.dot(vb, p_scr[slot, 2 * si + mp], preferred_element_type=F32)
                 for mp in range(2)]
            l1, l2 = a[0][dv:dv + 1], a[1][dv:dv + 1]
            att = a[0][:dv] / l1 - lam * (a[1][:dv] / l2)
            ms = jnp.mean(att * att, axis=0, keepdims=True)
            y = att * lax.rsqrt(ms + LN_EPS) * g
            o_ref[si * sq:(si + 1) * sq, :] = (y * (1.0 - LAMBDA_INIT)).T.astype(o_ref.dtype)

    acc_scr[...] = jnp.zeros_like(acc_scr)
    p_scr[1] = jnp.zeros(p_scr.shape[1:], BF16)
    logits_to_scratch(0, 0)
    ml0 = (jnp.full((1, sq), NEG_BIG, F32),) * n_ch

    def pair(i, ml):
        ml = step(2 * i, 0, ml, True, False)
        return step(2 * i + 1, 1, ml, True, False)

    ml = lax.fori_loop(0, n_full // 2, pair, ml0)

    @pl.when(n_full % 2 == 0)
    def _():
        finalize(n_full, 0, step(n_full, 0, ml, False, True))

    @pl.when(n_full % 2 == 1)
    def _():
        ml1 = step(n_full - 1, 0, ml, True, False)
        finalize(n_full, 1, step(n_full, 1, ml1, False, True))


def _diff_attn(qt, qaug, k3, vt3, slopes2, lq1, lk1, lq2, lk2, gb):
    b, s, _ = k3.shape
    dv = qt.shape[0] // DA_HEADS
    kw = k3.shape[2] // DA_HEADS
    dva = vt3.shape[1] // DA_HEADS
    w = dv * DA_HEADS
    tq, tk, sq = min(TQ, s), min(TK, s), min(SQ, s)
    assert tq == tk
    nq, nkb = s // tq, s // tk
    n_ch = 2 * (tq // sq)
    dk = lq1.shape[1]
    vec = pl.BlockSpec((1, dk), lambda bi, hi, qi: (0, 0))
    return pl.pallas_call(
        functools.partial(_diff_attn_kernel, tq=tq, tk=tk, sq=sq, rc=RC_ATTN),
        scratch_shapes=[pltpu.VMEM((2, n_ch, tk, sq), F32), pltpu.VMEM((2, n_ch, tk, sq), BF16),
                        pltpu.VMEM((tq // sq, tk, sq), F32), pltpu.VMEM((n_ch, dva, sq), F32)],
        grid=(b, DA_HEADS, nq),
        in_specs=[pl.BlockSpec(memory_space=pltpu.SMEM),
                  pl.BlockSpec((dv, tq), lambda bi, hi, qi: (hi, bi * nq + qi)),
                  pl.BlockSpec((None, kw - dv, sq), lambda bi, hi, qi: (hi, 0, 0)),
                  pl.BlockSpec((None, s, kw), lambda bi, hi, qi: (bi, 0, hi)),
                  pl.BlockSpec((nkb, dva, tk), lambda bi, hi, qi: (bi, hi, 0)),
                  vec, vec, vec, vec,
                  pl.BlockSpec((None, dv, LANES), lambda bi, hi, qi: (hi, 0, 0))],
        out_specs=pl.BlockSpec((None, tq, dv), lambda bi, hi, qi: (bi, qi, hi)),
        out_shape=jax.ShapeDtypeStruct((b, s, w), BF16),
        compiler_params=pltpu.CompilerParams(dimension_semantics=("arbitrary",) * 3,
                                             vmem_limit_bytes=VMEM_LIMIT),
        name="diff_attn",
    )(slopes2, qt, qaug, k3, vt3, lq1, lk1, lq2, lk2, gb)


def _mix_ln_kernel(ygm_ref, yda_ref, x_ref, wo_ref, g_ref, b_ref, x1_ref, x1b_ref, x1t_ref):
    gw = ygm_ref.shape[1]
    mix = (jnp.dot(ygm_ref[...], wo_ref[:gw, :], preferred_element_type=F32)
           + jnp.dot(yda_ref[...], wo_ref[gw:, :], preferred_element_type=F32))
    x1 = _layer_norm_rows(ALPHA * x_ref[...] + mix, g_ref[...], b_ref[...])
    x1_ref[...] = x1
    x1b_ref[...] = x1.astype(BF16)
    x1t_ref[...] = x1.T.astype(BF16)


def _mix_ln(ygm, yda, x2, wo_b, ln_g, ln_b):
    t, d = x2.shape
    tm = min(TM_PROJ, t)
    gw, dw = ygm.shape[1], yda.shape[1]
    const = lambda shape: pl.BlockSpec(shape, lambda i: (0,) * len(shape))
    row_blk = lambda w: pl.BlockSpec((tm, w), lambda i: (i, 0))
    return pl.pallas_call(
        _mix_ln_kernel,
        grid=(t // tm,),
        in_specs=[row_blk(gw), row_blk(dw), row_blk(d), const(wo_b.shape), const((1, d)), const((1, d))],
        out_specs=[row_blk(d), row_blk(d), pl.BlockSpec((d, tm), lambda i: (0, i))],
        out_shape=[jax.ShapeDtypeStruct((t, d), F32), jax.ShapeDtypeStruct((t, d), BF16),
                   jax.ShapeDtypeStruct((d, t), BF16)],
        compiler_params=pltpu.CompilerParams(dimension_semantics=("arbitrary",),
                                             vmem_limit_bytes=VMEM_LIMIT),
        name="mix_ln",
    )(ygm, yda, x2, wo_b, ln_g, ln_b)


def _top16_exact(val, rank_ref, vals_ref):
    key_iota = lax.broadcasted_iota(jnp.int32, val.shape, 0).astype(F32)
    rank = jnp.full(val.shape, NOT_RANKED, F32)
    for r in range(PEER_TOPK):
        m = jnp.max(val, axis=0, keepdims=True)
        idx = jnp.min(jnp.where(val == m, key_iota, float(N_KEYS)), axis=0, keepdims=True)
        hit = key_iota == idx
        rank = jnp.where(hit, float(r), rank)
        val = jnp.where(hit, -jnp.inf, val)
        vals_ref[r:r + 1, :] = m
    rank_ref[...] = rank


def _top16_if_distinct(val, rank_ref, vals_ref):
    rank = jnp.full(val.shape, NOT_RANKED, F32)
    for r in range(PEER_TOPK):
        m = jnp.max(val, axis=0, keepdims=True)
        hit = val == m
        rank = jnp.where(hit, float(r), rank)
        val = jnp.where(hit, -jnp.inf, val)
        vals_ref[r:r + 1, :] = m
    rank_ref[...] = rank
    return jnp.sum(jnp.where(val == -jnp.inf, 1.0, 0.0), axis=0, keepdims=True)


_PIECES = 10
_J_LIMIT = (8, 8, 8, 5, 4, 3, 2, 2, 2, 8)


def _candidates(va, vb):
    tl = va.shape[1]
    j8 = lax.broadcasted_iota(jnp.int32, (8, tl), 0).astype(F32)
    pieces, flats = [], []
    for p in range(_PIECES):
        if p == 0:
            c, f = vb[0:8] + va[0:1], j8
        elif p == 1:
            c, f = vb[8:16] + va[0:1], j8 + 8.0
        elif p == 9:
            c, f = va[8:16] + vb[0:1], 128.0 + 16.0 * j8
        else:
            i = p - 1
            c, f = vb[0:8] + va[i:i + 1], j8 + 16.0 * i
            c = jnp.where(j8 < float(_J_LIMIT[p]), c, -jnp.inf)
        pieces.append(c)
        flats.append(f)
    return jnp.concatenate(pieces, axis=0), jnp.concatenate(flats, axis=0)


def _joint_exact(cand, flat, s0, cnt_scr, z_scr):
    tl = cand.shape[1]
    i16 = lax.broadcasted_iota(jnp.int32, (PEER_TOPK, tl), 0).astype(F32)
    cnt = jnp.zeros((PEER_TOPK, tl), F32)
    z = jnp.zeros((1, tl), F32)
    for _ in range(PEER_TOPK):
        m = jnp.max(cand, axis=0, keepdims=True)
        idx = jnp.min(jnp.where(cand == m, flat, 1024.0), axis=0, keepdims=True)
        cand = jnp.where(flat == idx, -jnp.inf, cand)
        cnt = cnt + jnp.where(i16 == jnp.floor(idx * (1.0 / PEER_TOPK)), 1.0, 0.0)
        z = z + jnp.exp(m - s0)
    cnt_scr[...] = cnt
    z_scr[0:1, :] = z


def _joint_if_distinct(cand, s0, cnt_scr, z_scr):
    taken = jnp.zeros(cand.shape, F32)
    z = jnp.zeros((1, cand.shape[1]), F32)
    for _ in range(PEER_TOPK):
        m = jnp.max(cand, axis=0, keepdims=True)
        hit = cand == m
        taken = jnp.where(hit, 1.0, taken)
        cand = jnp.where(hit, -jnp.inf, cand)
        z = z + jnp.exp(m - s0)
    rowsum = lambda lo, hi: jnp.sum(taken[lo:hi], axis=0, keepdims=True)
    cnt_scr[0:1, :] = rowsum(0, 16)
    for i in range(1, 8):
        cnt_scr[i:i + 1, :] = rowsum(8 * (i + 1), 8 * (i + 2))
    cnt_scr[8:16, :] = taken[72:80]
    z_scr[0:1, :] = z
    return jnp.sum(taken, axis=0, keepdims=True)


def _key_scores(x1b, wq, ka, kb):
    half = ka.shape[1]
    qa = jnp.dot(x1b, wq[:, :half], preferred_element_type=F32).astype(BF16)
    qb = jnp.dot(x1b, wq[:, half:], preferred_element_type=F32).astype(BF16)
    return _nt_dot(ka, qa), _nt_dot(kb, qb)


def _peer_select_kernel(x1f_ref, wqf_ref, kaf_ref, kbf_ref, x1n_ref, wqn_ref, kan_ref, kbn_ref,
                        cnt_ref, wa_ref, rb_ref, wb_ref,
                        sc_scr, va_ref, vb_ref, ra_scr, rb_scr, cnt_scr, z_scr):
    i = pl.program_id(0)
    h = pl.program_id(1)
    slot = h % 2

    @pl.when((i == 0) & (h == 0))
    def _():
        sa0, sb0 = _key_scores(x1f_ref[...], wqf_ref[...], kaf_ref[...], kbf_ref[...])
        sc_scr[0, 0] = sa0
        sc_scr[0, 1] = sb0

    sa = sc_scr[slot, 0]
    sb = sc_scr[slot, 1]
    sa_next, sb_next = _key_scores(x1n_ref[...], wqn_ref[...], kan_ref[...], kbn_ref[...])
    sc_scr[1 - slot, 0] = sa_next
    sc_scr[1 - slot, 1] = sb_next

    ea = _top16_if_distinct(sa, ra_scr, va_ref)
    eb = _top16_if_distinct(sb, rb_scr, vb_ref)
    va = va_ref[...]
    vb = vb_ref[...]
    cand, _ = _candidates(va, vb)
    ec = _joint_if_distinct(cand, va[0:1] + vb[0:1], cnt_scr, z_scr)
    extracted = jnp.maximum(jnp.maximum(ea, eb), ec)

    @pl.when(jnp.max(extracted) > float(PEER_TOPK))
    def _():
        _top16_exact(sa, ra_scr, va_ref)
        _top16_exact(sb, rb_scr, vb_ref)
        va_x = va_ref[...]
        vb_x = vb_ref[...]
        cand_x, flat_x = _candidates(va_x, vb_x)
        _joint_exact(cand_x, flat_x, va_x[0:1] + vb_x[0:1], cnt_scr, z_scr)

    ra = ra_scr[...].astype(BF16)
    cnt = cnt_scr[...].astype(BF16)
    cnt_key = jnp.zeros(ra.shape, BF16)
    for r in range(PEER_TOPK):
        cnt_key = jnp.where(ra == float(r), cnt[r:r + 1], cnt_key)
    cnt_ref[...] = cnt_key.astype(F32)
    z = z_scr[0:1, :]
    wa_ref[...] = jnp.exp(sa - va_ref[0:1, :]) * (0.5 / z)
    rb_ref[...] = rb_scr[...].astype(BF16)
    wb_ref[...] = jnp.exp(sb - vb_ref[0:1, :]).astype(BF16)


def _peer_select(x1b, wq_b, ka_b, kb_b):
    t, d = x1b.shape
    nh, nk, half = ka_b.shape
    assert nh % 2 == 0
    tl = min(TL_SEL, t)
    nt = t // tl
    qcols = 2 * half
    out_blk = pl.BlockSpec((None, nk, tl), lambda i, h: (h, 0, i))
    out_f32 = jax.ShapeDtypeStruct((nh, nk, t), F32)
    out_b16 = jax.ShapeDtypeStruct((nh, nk, t), BF16)
    first = [pl.BlockSpec((tl, d), lambda i, h: (0, 0)), pl.BlockSpec((d, qcols), lambda i, h: (0, 0)),
             pl.BlockSpec((None, nk, half), lambda i, h: (0, 0, 0)),
             pl.BlockSpec((None, nk, half), lambda i, h: (0, 0, 0))]
    nxt = [pl.BlockSpec((tl, d), lambda i, h: (jnp.minimum(i + (h + 1) // nh, nt - 1), 0)),
           pl.BlockSpec((d, qcols), lambda i, h: (0, (h + 1) % nh)),
           pl.BlockSpec((None, nk, half), lambda i, h: ((h + 1) % nh, 0, 0)),
           pl.BlockSpec((None, nk, half), lambda i, h: ((h + 1) % nh, 0, 0))]
    return pl.pallas_call(
        _peer_select_kernel,
        grid=(nt, nh),
        in_specs=first + nxt,
        out_specs=[out_blk, out_blk, out_blk, out_blk],
        out_shape=[out_f32, out_f32, out_b16, out_b16],
        scratch_shapes=[pltpu.VMEM((2, 2, nk, tl), F32),
                        pltpu.VMEM((PEER_TOPK, tl), F32), pltpu.VMEM((PEER_TOPK, tl), F32),
                        pltpu.VMEM((nk, tl), F32), pltpu.VMEM((nk, tl), F32),
                        pltpu.VMEM((PEER_TOPK, tl), F32), pltpu.VMEM((SUBLANES, tl), F32)],
        compiler_params=pltpu.CompilerParams(dimension_semantics=("arbitrary", "arbitrary"),
                                             vmem_limit_bytes=VMEM_LIMIT),
        name="peer_select",
    )(x1b, wq_b, ka_b, kb_b, x1b, wq_b, ka_b, kb_b)


def _peer_dense_kernel(x1t_ref, u_ref, v_ref, cnt_ref, wa_ref, rb_ref, wb_ref, o_ref, s_ref, *, sub):
    j = pl.program_id(1)
    eb = u_ref.shape[0]
    tt = x1t_ref.shape[1]
    n_sub = tt // sub
    n_slot = s_ref.shape[0]
    a_rows = eb // N_KEYS
    a_grp = 2
    n_rg = N_KEYS // SUBLANES
    rg_grp = 8
    zero = jnp.zeros((), BF16)

    @pl.when(j == 0)
    def _():
        o_ref[...] = jnp.zeros_like(o_ref)

    def pre_act(c):
        s_ref[c % n_slot] = jnp.dot(u_ref[...], x1t_ref[:, c * sub:(c + 1) * sub], preferred_element_type=F32)

    def gate_and_activate(c):
        ts = slice(c * sub, (c + 1) * sub)
        cnt_b = [cnt_ref[hh, :, ts].astype(BF16) for hh in range(PEER_HEADS)]
        wa_b = [wa_ref[hh, :, ts].astype(BF16) for hh in range(PEER_HEADS)]
        pieces = [None] * (eb // SUBLANES)
        for a0 in range(0, a_rows, a_grp):
            for r0 in range(0, n_rg, rg_grp):
                g = [[jnp.zeros((SUBLANES, sub), BF16) for _ in range(rg_grp)] for _ in range(a_grp)]
                for hh in range(PEER_HEADS):
                    cnt = [jnp.broadcast_to(cnt_b[hh][a0 + k:a0 + k + 1], (SUBLANES, sub)) for k in range(a_grp)]
                    wa = [jnp.broadcast_to(wa_b[hh][a0 + k:a0 + k + 1], (SUBLANES, sub)) for k in range(a_grp)]
                    for r in range(rg_grp):
                        bs = slice((r0 + r) * SUBLANES, (r0 + r + 1) * SUBLANES)
                        rb = rb_ref[hh, bs, ts]
                        wb = wb_ref[hh, bs, ts]
                        for k in range(a_grp):
                            g[k][r] = g[k][r] + jnp.where(rb < cnt[k], wb * wa[k], zero)
                for k in range(a_grp):
                    for r in range(rg_grp):
                        e0 = (a0 + k) * N_KEYS + (r0 + r) * SUBLANES
                        x = s_ref[c % n_slot, e0:e0 + SUBLANES, :]
                        act = x * (1.0 + lax.erf(x * (1.0 / math.sqrt(2.0))))
                        pieces[e0 // SUBLANES] = act.astype(BF16) * g[k][r]
        return jnp.concatenate(pieces, axis=0)

    pre_act(0)
    for c in range(n_sub):
        if c + 1 < n_sub:
            pre_act(c + 1)
        hact = gate_and_activate(c)
        o_ref[:, c * sub:(c + 1) * sub] += lax.dot_general(v_ref[...], hact, (((0,), (0,)), ((), ())),
                                                          preferred_element_type=F32)


def _peer_dense(x1t, u_b, v_b, cnt, wa, rb, wb):
    d, t = x1t.shape
    ne = u_b.shape[0]
    tt = min(TT_DENSE, t)
    sub = min(SUB_DENSE, tt)
    eb = EB_DENSE
    a_rows = eb // N_KEYS
    row_meta = pl.BlockSpec((PEER_HEADS, a_rows, tt), lambda i, j: (0, j, i))
    key_meta = pl.BlockSpec((PEER_HEADS, N_KEYS, tt), lambda i, j: (0, 0, i))
    return pl.pallas_call(
        functools.partial(_peer_dense_kernel, sub=sub),
        grid=(t // tt, ne // eb),
        in_specs=[pl.BlockSpec((d, tt), lambda i, j: (0, i)),
                  pl.BlockSpec((eb, d), lambda i, j: (j, 0)),
                  pl.BlockSpec((eb, d), lambda i, j: (j, 0)),
                  row_meta, row_meta, key_meta, key_meta],
        out_specs=pl.BlockSpec((d, tt), lambda i, j: (0, i)),
        out_shape=jax.ShapeDtypeStruct((d, t), F32),
        scratch_shapes=[pltpu.VMEM((min(2, tt // sub), eb, sub), F32)],
        compiler_params=pltpu.CompilerParams(dimension_semantics=("arbitrary", "arbitrary"),
                                             vmem_limit_bytes=VMEM_LIMIT_DENSE),
        name="peer_dense",
    )(x1t, u_b, v_b, cnt, wa, rb, wb)


def _ffn_ln_kernel(ffnt_ref, x1_ref, g_ref, b_ref, o_ref):
    ffn = ffnt_ref[...].T
    o_ref[...] = _layer_norm_rows(ALPHA * x1_ref[...] + ffn, g_ref[...], b_ref[...])


def _ffn_ln(ffnt, x1, ln_g, ln_b):
    t, d = x1.shape
    tm = min(TM_PROJ, t)
    const = lambda shape: pl.BlockSpec(shape, lambda i: (0,) * len(shape))
    return pl.pallas_call(
        _ffn_ln_kernel,
        grid=(t // tm,),
        in_specs=[pl.BlockSpec((d, tm), lambda i: (0, i)), pl.BlockSpec((tm, d), lambda i: (i, 0)),
                  const((1, d)), const((1, d))],
        out_specs=pl.BlockSpec((tm, d), lambda i: (i, 0)),
        out_shape=jax.ShapeDtypeStruct((t, d), F32),
        compiler_params=pltpu.CompilerParams(dimension_semantics=("arbitrary",),
                                             vmem_limit_bytes=VMEM_LIMIT),
        name="ffn_ln",
    )(ffnt, x1, ln_g, ln_b)


def _alibi_features(slopes2, tk, sq):
    assert tk <= 512 and sq <= 256
    c = jnp.asarray(slopes2, F32)
    nh = c.shape[0]
    hi = c.astype(BF16).astype(F32)
    mid = (c - hi).astype(BF16).astype(F32)
    lo = (c - hi - mid).astype(BF16).astype(F32)
    parts = jnp.stack([hi, mid, lo], axis=1)
    kr = np.arange(tk)
    kfeat = jnp.zeros((nh, tk, FEAT_WIDTH), F32)
    kfeat = kfeat.at[:, :, 0:3].set((kr % 256).astype(np.float32)[None, :, None])
    kfeat = kfeat.at[:, :, 3:6].set((256 * (kr // 256)).astype(np.float32)[None, :, None])
    kfeat = kfeat.at[:, :, 6:9].set(jnp.broadcast_to(parts[:, None, :], (nh, tk, 3)))
    qfeat = jnp.zeros((nh, FEAT_WIDTH, sq), F32)
    qfeat = qfeat.at[:, 0:3, :].set(jnp.broadcast_to(parts[:, :, None], (nh, 3, sq)))
    qfeat = qfeat.at[:, 3:6, :].set(jnp.broadcast_to(parts[:, :, None], (nh, 3, sq)))
    qfeat = qfeat.at[:, 6:9, :].set(jnp.broadcast_to(-jnp.arange(sq, dtype=F32)[None, None, :], (nh, 3, sq)))
    return kfeat.astype(BF16), qfeat.astype(BF16)


def kernel(x, w_in, gm_norm_g, gm_norm_b, gm_w_s, gm_b_s, lam_q1, lam_k1, lam_q2, lam_k2, da_norm_g,
           w_o, ln1_g, ln1_b, peer_w_q, peer_keys_a, peer_keys_b, peer_u, peer_v, ln2_g, ln2_b):
    b, s, d = x.shape
    t = b * s
    gm_width = gm_norm_g.shape[0]
    da_width = da_norm_g.shape[0] * da_norm_g.shape[1]
    dk = lam_q1.shape[0]
    assert s % CHUNK == 0 and gm_w_s.shape == (GM_HEADS, CHUNK, CHUNK)
    assert peer_u.shape[0] == N_KEYS * N_KEYS and peer_keys_a.shape[:2] == (PEER_HEADS, N_KEYS)

    row = lambda v: v.reshape(1, -1).astype(F32)
    x2 = x.reshape(t, d)
    bs_tile = jnp.repeat(gm_b_s.T, gm_width // GM_HEADS, axis=1)
    q_scale = (dk ** -0.5) * LOG2E
    assert s % TK == 0
    slopes2 = jnp.asarray(2.0 ** (-(8.0 / DA_HEADS) * np.arange(1, DA_HEADS + 1)) * LOG2E, F32)
    kfeat, qfeat = _alibi_features(slopes2, TK, SQ)
    ygm, qt, k, vt3 = _proj_gmlp(x2, w_in.astype(BF16), kfeat, row(gm_norm_g), row(gm_norm_b), gm_w_s,
                                 bs_tile, gm_width, da_width, q_scale)

    g_lanes = jnp.broadcast_to(da_norm_g.astype(F32)[:, :, None], da_norm_g.shape + (LANES,))
    yda = _diff_attn(qt, qfeat, k.reshape(b, s, -1), vt3, slopes2, row(lam_q1), row(lam_k1), row(lam_q2),
                     row(lam_k2), g_lanes)

    x1, x1b, x1t = _mix_ln(ygm, yda.reshape(t, da_width), x2, w_o.astype(BF16), row(ln1_g), row(ln1_b))
    cnt, wa, rb, wb = _peer_select(x1b, peer_w_q.astype(BF16), peer_keys_a.astype(BF16),
                                   peer_keys_b.astype(BF16))
    ffnt = _peer_dense(x1t, peer_u.astype(BF16), peer_v.astype(BF16), cnt, wa, rb, wb)
    out = _ffn_ln(ffnt, x1, row(ln2_g), row(ln2_b))
    return out.reshape(b, s, d)
```

```python
import functools
import math

import jax
import jax.numpy as jnp
import numpy as np
from jax import lax
from jax.experimental import pallas as pl
from jax.experimental.pallas import tpu as pltpu

F32 = jnp.float32
BF16 = jnp.bfloat16

GM_HEADS = 4
CHUNK = 128
DA_HEADS = 4
LAMBDA_INIT = 0.8 - 0.6 * math.exp(-0.3 * (1 - 1))
PEER_HEADS = 8
N_KEYS = 128
PEER_TOPK = 16
DEPTH = 1
ALPHA = (2.0 * DEPTH) ** 0.25
LN_EPS = 1e-5
NEG_BIG = -1e30
LOG2E = 1.4426950408889634
NOT_RANKED = 64.0

LANES = 128
SUBLANES = 8
VMEM_LIMIT = 48 * 1024 * 1024
VMEM_LIMIT_DENSE = 58 * 1024 * 1024
TM_PROJ = 1024
TQ = 512
SQ = 256
TK = 512
RC_ATTN = 64
ONES_ROWS = 16
FEAT_WIDTH = 128
TL_SEL = 512
TT_DENSE = 1024
SUB_DENSE = 1024
EB_DENSE = 2048


def _gelu_exact(x):
    return 0.5 * x * (1.0 + lax.erf(x * (1.0 / math.sqrt(2.0))))


def _layer_norm_rows(x, g, b):
    mu = jnp.mean(x, axis=-1, keepdims=True)
    xc = x - mu
    var = jnp.mean(xc * xc, axis=-1, keepdims=True)
    return xc * lax.rsqrt(var + LN_EPS) * g + b


def _nt_dot(a, b):
    return lax.dot_general(a, b, (((1,), (1,)), ((), ())), preferred_element_type=F32)


def _proj_gmlp_kernel(x_ref, w_ref, wqt_ref, wvt_ref, kfeat_ref, g_ref, b_ref, ws_ref, bs_ref,
                      ygm_ref, qt_ref, k_ref, vt_ref, *, gm_width, da_width, q_scale):
    xb = x_ref[...].astype(BF16)
    tm = xb.shape[0]
    hd = gm_width // GM_HEADS
    u = _gelu_exact(jnp.dot(xb, w_ref[:, :gm_width], preferred_element_type=F32))
    z = _gelu_exact(jnp.dot(xb, w_ref[:, gm_width:2 * gm_width], preferred_element_type=F32))
    vn = _layer_norm_rows(z, g_ref[...], b_ref[...]).astype(BF16)
    row = lax.broadcasted_iota(jnp.int32, (CHUNK, CHUNK), 0)
    col = lax.broadcasted_iota(jnp.int32, (CHUNK, CHUNK), 1)
    causal = col <= row
    for hh in range(GM_HEADS):
        w_c = jnp.where(causal, ws_ref[hh], 0.0).astype(BF16)
        for c in range(tm // CHUNK):
            rs = slice(c * CHUNK, (c + 1) * CHUNK)
            cs = slice(hh * hd, (hh + 1) * hd)
            mixed = jnp.dot(w_c, vn[rs, cs], preferred_element_type=F32) + bs_ref[:, cs]
            ygm_ref[rs, cs] = (u[rs, cs] * mixed).astype(BF16)
    o = 2 * gm_width
    dvh = da_width // DA_HEADS
    fw = kfeat_ref.shape[2]
    k = jnp.dot(xb, w_ref[:, o + da_width:o + 2 * da_width], preferred_element_type=F32).astype(BF16)
    qt_ref[...] = (_nt_dot(wqt_ref[...], xb) * q_scale).astype(BF16)
    vt = _nt_dot(wvt_ref[...], xb).astype(BF16)
    ones = jnp.ones((ONES_ROWS, tm), BF16)
    for hh in range(DA_HEADS):
        k_ref[:, hh * (dvh + fw):hh * (dvh + fw) + dvh] = k[:, hh * dvh:(hh + 1) * dvh]
        k_ref[:, hh * (dvh + fw) + dvh:(hh + 1) * (dvh + fw)] = kfeat_ref[hh]
        r0 = hh * (dvh + ONES_ROWS)
        vt_ref[r0:r0 + dvh, :] = vt[hh * dvh:(hh + 1) * dvh]
        vt_ref[r0 + dvh:r0 + dvh + ONES_ROWS, :] = ones


def _proj_gmlp(x2, w_in_b, kfeat, gm_g, gm_b, w_s, bs_tile, gm_width, da_width, q_scale):
    t, d = x2.shape
    tm = TK
    cols = w_in_b.shape[1]
    kw = da_width + DA_HEADS * kfeat.shape[2]
    vr = da_width + DA_HEADS * ONES_ROWS
    o = 2 * gm_width
    wqt = w_in_b[:, o:o + da_width].T
    wvt = w_in_b[:, o + 2 * da_width:o + 3 * da_width].T
    const = lambda shape: pl.BlockSpec(shape, lambda i: (0,) * len(shape))
    row_blk = lambda w: pl.BlockSpec((tm, w), lambda i: (i, 0))
    return pl.pallas_call(
        functools.partial(_proj_gmlp_kernel, gm_width=gm_width, da_width=da_width, q_scale=q_scale),
        grid=(t // tm,),
        in_specs=[row_blk(d), const((d, cols)), const((da_width, d)), const((da_width, d)), const(kfeat.shape),
                  const((1, gm_width)), const((1, gm_width)),
                  const((GM_HEADS, CHUNK, CHUNK)), const((CHUNK, gm_width))],
        out_specs=[row_blk(gm_width), pl.BlockSpec((da_width, tm), lambda i: (0, i)), row_blk(kw),
                   pl.BlockSpec((None, vr, tm), lambda i: (i, 0, 0))],
        out_shape=[jax.ShapeDtypeStruct((t, gm_width), BF16), jax.ShapeDtypeStruct((da_width, t), BF16),
                   jax.ShapeDtypeStruct((t, kw), BF16),
                   jax.ShapeDtypeStruct((t // tm, vr, tm), BF16)],
        compiler_params=pltpu.CompilerParams(dimension_semantics=("arbitrary",),
                                             vmem_limit_bytes=VMEM_LIMIT),
        name="proj_gmlp",
    )(x2, w_in_b, wqt, wvt, kfeat, gm_g, gm_b, w_s, bs_tile)


def _diff_attn_kernel(slope_ref, qt_ref, qaug_ref, k_ref, vt_ref, lq1_ref, lk1_ref, lq2_ref, lk2_ref, gb_ref, o_ref,
                      s_scr, p_scr, mask_scr, acc_scr, *, tq, tk, sq, rc):
    h = pl.program_id(1)
    qi = pl.program_id(2)
    slope2 = slope_ref[h]
    dv = qt_ref.shape[0]
    dk = dv // 2
    n_sq = tq // sq
    n_ch = 2 * n_sq
    n_full = qi

    @pl.when(qi == 0)
    def _():
        kr = lax.broadcasted_iota(jnp.int32, (tk, sq), 0)
        qc = lax.broadcasted_iota(jnp.int32, (tk, sq), 1)
        for si in range(n_sq):
            mask_scr[si] = jnp.where(kr <= qc + si * sq, 0.0, NEG_BIG)

    rowq = lax.broadcasted_iota(jnp.int32, (dv, sq), 0)
    chains = []
    for si in range(n_sq):
        qt = qt_ref[:, si * sq:(si + 1) * sq]
        zero = jnp.zeros_like(qt)
        chains.append((si, jnp.concatenate([jnp.where(rowq < dk, qt, zero), qaug_ref[...]], axis=0)))
        chains.append((si, jnp.concatenate([jnp.where(rowq >= dk, qt, zero), qaug_ref[...]], axis=0)))

    def logits_to_scratch(kj, slot):
        kb = k_ref[pl.ds(pl.multiple_of(kj * tk, tk), tk), :]
        for ci, (_, qm) in enumerate(chains):
            s_scr[slot, ci] = jnp.dot(kb, qm, preferred_element_type=F32)

    def softmax_step(kj, slot, ml, diag):
        out, alphas = [], []
        for ci, (si, _) in enumerate(chains):
            m = ml[ci]
            rel = kj * tk - (qi * tq + si * sq)
            off = rel.astype(F32) * slope2

            def logits(r0):
                t = s_scr[slot, ci, r0:r0 + rc, :]
                return t + mask_scr[si, r0:r0 + rc, :] if diag else t

            cmax = logits(0)
            for r0 in range(rc, tk, rc):
                cmax = jnp.maximum(cmax, logits(r0))
            m_new = jnp.maximum(m, jnp.max(cmax, axis=0, keepdims=True) + off)
            shift = off - m_new
            for r0 in range(0, tk, rc):
                p_scr[slot, ci, r0:r0 + rc, :] = jnp.exp2(logits(r0) + shift).astype(BF16)
            out.append(m_new)
            alphas.append(jnp.exp2(m - m_new))
        return tuple(out), alphas

    def accumulate_previous(kj_prev, slot_prev, alphas):
        vb = vt_ref[kj_prev]
        for ci in range(n_ch):
            pv = jnp.dot(vb, p_scr[slot_prev, ci], preferred_element_type=F32)
            acc_scr[ci] = alphas[ci] * (acc_scr[ci] + pv)

    def step(kj, slot, ml, prefetch_logits, diag):
        if prefetch_logits:
            logits_to_scratch(kj + 1, 1 - slot)
        ml, alphas = softmax_step(kj, slot, ml, diag)
        accumulate_previous(jnp.maximum(kj - 1, 0), 1 - slot, alphas)
        return ml

    def finalize(kj, slot, ml):
        vb = vt_ref[kj]
        lam = (jnp.exp(jnp.sum(lq1_ref[...] * lk1_ref[...], axis=1, keepdims=True))
               - jnp.exp(jnp.sum(lq2_ref[...] * lk2_ref[...], axis=1, keepdims=True)) + LAMBDA_INIT)
        g = jnp.concatenate([gb_ref[...]] * (sq // gb_ref.shape[1]), axis=1)
        for si in range(n_sq):
            a = [acc_scr[2 * si + mp] + jnp.dot(vb, p_scr[slot, 2 * si + mp], preferred_element_type=F32)
                 for mp in range(2)]
            l1, l2 = a[0][dv:dv + 1], a[1][dv:dv + 1]
            att = a[0][:dv] / l1 - lam * (a[1][:dv] / l2)
            ms = jnp.mean(att * att, axis=0, keepdims=True)
            y = att * lax.rsqrt(ms + LN_EPS) * g
            o_ref[si * sq:(si + 1) * sq, :] = (y * (1.0 - LAMBDA_INIT)).T.astype(o_ref.dtype)

    acc_scr[...] = jnp.zeros_like(acc_scr)
    p_scr[1] = jnp.zeros(p_scr.shape[1:], BF16)
    logits_to_scratch(0, 0)
    ml0 = (jnp.full((1, sq), NEG_BIG, F32),) * n_ch

    def pair(i, ml):
        ml = step(2 * i, 0, ml, True, False)
        return step(2 * i + 1, 1, ml, True, False)

    ml = lax.fori_loop(0, n_full // 2, pair, ml0)

    @pl.when(n_full % 2 == 0)
    def _():
        finalize(n_full, 0, step(n_full, 0, ml, False, True))

    @pl.when(n_full % 2 == 1)
    def _():
        ml1 = step(n_full - 1, 0, ml, True, False)
        finalize(n_full, 1, step(n_full, 1, ml1, False, True))


def _diff_attn(qt, qaug, k3, vt3, slopes2, lq1, lk1, lq2, lk2, gb):
    b, s, _ = k3.shape
    dv = qt.shape[0] // DA_HEADS
    kw = k3.shape[2] // DA_HEADS
    dva = vt3.shape[1] // DA_HEADS
    w = dv * DA_HEADS
    tq, tk, sq = min(TQ, s), min(TK, s), min(SQ, s)
    assert tq == tk
    nq, nkb = s // tq, s // tk
    n_ch = 2 * (tq // sq)
    dk = lq1.shape[1]
    vec = pl.BlockSpec((1, dk), lambda bi, hi, qi: (0, 0))
    return pl.pallas_call(
        functools.partial(_diff_attn_kernel, tq=tq, tk=tk, sq=sq, rc=RC_ATTN),
        scratch_shapes=[pltpu.VMEM((2, n_ch, tk, sq), F32), pltpu.VMEM((2, n_ch, tk, sq), BF16),
                        pltpu.VMEM((tq // sq, tk, sq), F32), pltpu.VMEM((n_ch, dva, sq), F32)],
        grid=(b, DA_HEADS, nq),
        in_specs=[pl.BlockSpec(memory_space=pltpu.SMEM),
                  pl.BlockSpec((dv, tq), lambda bi, hi, qi: (hi, bi * nq + qi)),
                  pl.BlockSpec((None, kw - dv, sq), lambda bi, hi, qi: (hi, 0, 0)),
                  pl.BlockSpec((None, s, kw), lambda bi, hi, qi: (bi, 0, hi)),
                  pl.BlockSpec((nkb, dva, tk), lambda bi, hi, qi: (bi, hi, 0)),
                  vec, vec, vec, vec,
                  pl.BlockSpec((None, dv, LANES), lambda bi, hi, qi: (hi, 0, 0))],
        out_specs=pl.BlockSpec((None, tq, dv), lambda bi, hi, qi: (bi, qi, hi)),
        out_shape=jax.ShapeDtypeStruct((b, s, w), BF16),
        compiler_params=pltpu.CompilerParams(dimension_semantics=("arbitrary",) * 3,
                                             vmem_limit_bytes=VMEM_LIMIT),
        name="diff_attn",
    )(slopes2, qt, qaug, k3, vt3, lq1, lk1, lq2, lk2, gb)


def _mix_ln_kernel(ygm_ref, yda_ref, x_ref, wo_ref, g_ref, b_ref, x1_ref, x1b_ref, x1t_ref):
    gw = ygm_ref.shape[1]
    mix = (jnp.dot(ygm_ref[...], wo_ref[:gw, :], preferred_element_type=F32)
           + jnp.dot(yda_ref[...], wo_ref[gw:, :], preferred_element_type=F32))
    x1 = _layer_norm_rows(ALPHA * x_ref[...] + mix, g_ref[...], b_ref[...])
    x1_ref[...] = x1
    x1b_ref[...] = x1.astype(BF16)
    x1t_ref[...] = x1.T.astype(BF16)


def _mix_ln(ygm, yda, x2, wo_b, ln_g, ln_b):
    t, d = x2.shape
    tm = min(TM_PROJ, t)
    gw, dw = ygm.shape[1], yda.shape[1]
    const = lambda shape: pl.BlockSpec(shape, lambda i: (0,) * len(shape))
    row_blk = lambda w: pl.BlockSpec((tm, w), lambda i: (i, 0))
    return pl.pallas_call(
        _mix_ln_kernel,
        grid=(t // tm,),
        in_specs=[row_blk(gw), row_blk(dw), row_blk(d), const(wo_b.shape), const((1, d)), const((1, d))],
        out_specs=[row_blk(d), row_blk(d), pl.BlockSpec((d, tm), lambda i: (0, i))],
        out_shape=[jax.ShapeDtypeStruct((t, d), F32), jax.ShapeDtypeStruct((t, d), BF16),
                   jax.ShapeDtypeStruct((d, t), BF16)],
        compiler_params=pltpu.CompilerParams(dimension_semantics=("arbitrary",),
                                             vmem_limit_bytes=VMEM_LIMIT),
        name="mix_ln",
    )(ygm, yda, x2, wo_b, ln_g, ln_b)


def _top16_exact(val, rank_ref, vals_ref):
    key_iota = lax.broadcasted_iota(jnp.int32, val.shape, 0).astype(F32)
    rank = jnp.full(val.shape, NOT_RANKED, F32)
    for r in range(PEER_TOPK):
        m = jnp.max(val, axis=0, keepdims=True)
        idx = jnp.min(jnp.where(val == m, key_iota, float(N_KEYS)), axis=0, keepdims=True)
        hit = key_iota == idx
        rank = jnp.where(hit, float(r), rank)
        val = jnp.where(hit, -jnp.inf, val)
        vals_ref[r:r + 1, :] = m
    rank_ref[...] = rank


def _top16_if_distinct(val, rank_ref, vals_ref):
    rank = jnp.full(val.shape, NOT_RANKED, F32)
    for r in range(PEER_TOPK):
        m = jnp.max(val, axis=0, keepdims=True)
        hit = val == m
        rank = jnp.where(hit, float(r), rank)
        val = jnp.where(hit, -jnp.inf, val)
        vals_ref[r:r + 1, :] = m
    rank_ref[...] = rank
    return jnp.sum(jnp.where(val == -jnp.inf, 1.0, 0.0), axis=0, keepdims=True)


_PIECES = 10
_J_LIMIT = (8, 8, 8, 5, 4, 3, 2, 2, 2, 8)


def _candidates(va, vb):
    tl = va.shape[1]
    j8 = lax.broadcasted_iota(jnp.int32, (8, tl), 0).astype(F32)
    pieces, flats = [], []
    for p in range(_PIECES):
        if p == 0:
            c, f = vb[0:8] + va[0:1], j8
        elif p == 1:
            c, f = vb[8:16] + va[0:1], j8 + 8.0
        elif p == 9:
            c, f = va[8:16] + vb[0:1], 128.0 + 16.0 * j8
        else:
            i = p - 1
            c, f = vb[0:8] + va[i:i + 1], j8 + 16.0 * i
            c = jnp.where(j8 < float(_J_LIMIT[p]), c, -jnp.inf)
        pieces.append(c)
        flats.append(f)
    return jnp.concatenate(pieces, axis=0), jnp.concatenate(flats, axis=0)


def _joint_exact(cand, flat, s0, cnt_scr, z_scr):
    tl = cand.shape[1]
    i16 = lax.broadcasted_iota(jnp.int32, (PEER_TOPK, tl), 0).astype(F32)
    cnt = jnp.zeros((PEER_TOPK, tl), F32)
    z = jnp.zeros((1, tl), F32)
    for _ in range(PEER_TOPK):
        m = jnp.max(cand, axis=0, keepdims=True)
        idx = jnp.min(jnp.where(cand == m, flat, 1024.0), axis=0, keepdims=True)
        cand = jnp.where(flat == idx, -jnp.inf, cand)
        cnt = cnt + jnp.where(i16 == jnp.floor(idx * (1.0 / PEER_TOPK)), 1.0, 0.0)
        z = z + jnp.exp(m - s0)
    cnt_scr[...] = cnt
    z_scr[0:1, :] = z


def _joint_if_distinct(cand, s0, cnt_scr, z_scr):
    taken = jnp.zeros(cand.shape, F32)
    z = jnp.zeros((1, cand.shape[1]), F32)
    for _ in range(PEER_TOPK):
        m = jnp.max(cand, axis=0, keepdims=True)
        hit = cand == m
        taken = jnp.where(hit, 1.0, taken)
        cand = jnp.where(hit, -jnp.inf, cand)
        z = z + jnp.exp(m - s0)
    rowsum = lambda lo, hi: jnp.sum(taken[lo:hi], axis=0, keepdims=True)
    cnt_scr[0:1, :] = rowsum(0, 16)
    for i in range(1, 8):
        cnt_scr[i:i + 1, :] = rowsum(8 * (i + 1), 8 * (i + 2))
    cnt_scr[8:16, :] = taken[72:80]
    z_scr[0:1, :] = z
    return jnp.sum(taken, axis=0, keepdims=True)


def _key_scores(x1b, wq, ka, kb):
    half = ka.shape[1]
    qa = jnp.dot(x1b, wq[:, :half], preferred_element_type=F32).astype(BF16)
    qb = jnp.dot(x1b, wq[:, half:], preferred_element_type=F32).astype(BF16)
    return _nt_dot(ka, qa), _nt_dot(kb, qb)


def _peer_select_kernel(x1f_ref, wqf_ref, kaf_ref, kbf_ref, x1n_ref, wqn_ref, kan_ref, kbn_ref, u_ref, v_ref,
                        cnt_ref, wa_ref, rb_ref, wb_ref, ub_ref, vb16_ref,
                        sc_scr, va_ref, vb_ref, ra_scr, rb_scr, cnt_scr, z_scr):
    ub_ref[...] = u_ref[...].astype(BF16)
    vb16_ref[...] = v_ref[...].astype(BF16)
    i = pl.program_id(0)
    h = pl.program_id(1)
    slot = h % 2

    @pl.when((i == 0) & (h == 0))
    def _():
        sa0, sb0 = _key_scores(x1f_ref[...], wqf_ref[...], kaf_ref[...], kbf_ref[...])
        sc_scr[0, 0] = sa0
        sc_scr[0, 1] = sb0

    sa = sc_scr[slot, 0]
    sb = sc_scr[slot, 1]
    sa_next, sb_next = _key_scores(x1n_ref[...], wqn_ref[...], kan_ref[...], kbn_ref[...])
    sc_scr[1 - slot, 0] = sa_next
    sc_scr[1 - slot, 1] = sb_next

    ea = _top16_if_distinct(sa, ra_scr, va_ref)
    eb = _top16_if_distinct(sb, rb_scr, vb_ref)
    va = va_ref[...]
    vb = vb_ref[...]
    cand, _ = _candidates(va, vb)
    ec = _joint_if_distinct(cand, va[0:1] + vb[0:1], cnt_scr, z_scr)
    extracted = jnp.maximum(jnp.maximum(ea, eb), ec)

    @pl.when(jnp.max(extracted) > float(PEER_TOPK))
    def _():
        _top16_exact(sa, ra_scr, va_ref)
        _top16_exact(sb, rb_scr, vb_ref)
        va_x = va_ref[...]
        vb_x = vb_ref[...]
        cand_x, flat_x = _candidates(va_x, vb_x)
        _joint_exact(cand_x, flat_x, va_x[0:1] + vb_x[0:1], cnt_scr, z_scr)

    ra = ra_scr[...].astype(BF16)
    cnt = cnt_scr[...].astype(BF16)
    cnt_key = jnp.zeros(ra.shape, BF16)
    for r in range(PEER_TOPK):
        cnt_key = jnp.where(ra == float(r), cnt[r:r + 1], cnt_key)
    cnt_ref[...] = cnt_key.astype(F32)
    z = z_scr[0:1, :]
    wa_ref[...] = jnp.exp(sa - va_ref[0:1, :]) * (0.5 / z)
    rb_ref[...] = rb_scr[...].astype(BF16)
    wb_ref[...] = jnp.exp(sb - vb_ref[0:1, :]).astype(BF16)


def _peer_select(x1b, wq_b, ka_b, kb_b, u_tab, v_tab):
    t, d = x1b.shape
    nh, nk, half = ka_b.shape
    assert nh % 2 == 0
    tl = min(TL_SEL, t)
    nt = t // tl
    ne, dt = u_tab.shape
    slab = ne // (nt * nh)
    assert slab * nt * nh == ne and slab % 16 == 0
    tab_blk = pl.BlockSpec((slab, dt), lambda i, h: (i * nh + h, 0))
    tab_b16 = jax.ShapeDtypeStruct((ne, dt), BF16)
    qcols = 2 * half
    out_blk = pl.BlockSpec((None, nk, tl), lambda i, h: (h, 0, i))
    out_f32 = jax.ShapeDtypeStruct((nh, nk, t), F32)
    out_b16 = jax.ShapeDtypeStruct((nh, nk, t), BF16)
    first = [pl.BlockSpec((tl, d), lambda i, h: (0, 0)), pl.BlockSpec((d, qcols), lambda i, h: (0, 0)),
             pl.BlockSpec((None, nk, half), lambda i, h: (0, 0, 0)),
             pl.BlockSpec((None, nk, half), lambda i, h: (0, 0, 0))]
    nxt = [pl.BlockSpec((tl, d), lambda i, h: (jnp.minimum(i + (h + 1) // nh, nt - 1), 0)),
           pl.BlockSpec((d, qcols), lambda i, h: (0, (h + 1) % nh)),
           pl.BlockSpec((None, nk, half), lambda i, h: ((h + 1) % nh, 0, 0)),
           pl.BlockSpec((None, nk, half), lambda i, h: ((h + 1) % nh, 0, 0))]
    return pl.pallas_call(
        _peer_select_kernel,
        grid=(nt, nh),
        in_specs=first + nxt + [tab_blk, tab_blk],
        out_specs=[out_blk, out_blk, out_blk, out_blk, tab_blk, tab_blk],
        out_shape=[out_f32, out_f32, out_b16, out_b16, tab_b16, tab_b16],
        scratch_shapes=[pltpu.VMEM((2, 2, nk, tl), F32),
                        pltpu.VMEM((PEER_TOPK, tl), F32), pltpu.VMEM((PEER_TOPK, tl), F32),
                        pltpu.VMEM((nk, tl), F32), pltpu.VMEM((nk, tl), F32),
                        pltpu.VMEM((PEER_TOPK, tl), F32), pltpu.VMEM((SUBLANES, tl), F32)],
        compiler_params=pltpu.CompilerParams(dimension_semantics=("arbitrary", "arbitrary"),
                                             vmem_limit_bytes=VMEM_LIMIT),
        name="peer_select",
    )(x1b, wq_b, ka_b, kb_b, x1b, wq_b, ka_b, kb_b, u_tab, v_tab)


def _peer_dense_kernel(x1t_ref, u_ref, v_ref, cnt_ref, wa_ref, rb_ref, wb_ref, o_ref, s_ref, *, sub):
    j = pl.program_id(1)
    eb = u_ref.shape[0]
    tt = x1t_ref.shape[1]
    n_sub = tt // sub
    n_slot = s_ref.shape[0]
    a_rows = eb // N_KEYS
    a_grp = 2
    n_rg = N_KEYS // SUBLANES
    rg_grp = 8
    zero = jnp.zeros((), BF16)

    @pl.when(j == 0)
    def _():
        o_ref[...] = jnp.zeros_like(o_ref)

    def pre_act(c):
        s_ref[c % n_slot] = jnp.dot(u_ref[...], x1t_ref[:, c * sub:(c + 1) * sub], preferred_element_type=F32)

    def gate_and_activate(c):
        ts = slice(c * sub, (c + 1) * sub)
        cnt_b = [cnt_ref[hh, :, ts].astype(BF16) for hh in range(PEER_HEADS)]
        wa_b = [wa_ref[hh, :, ts].astype(BF16) for hh in range(PEER_HEADS)]
        pieces = [None] * (eb // SUBLANES)
        for a0 in range(0, a_rows, a_grp):
            for r0 in range(0, n_rg, rg_grp):
                g = [[jnp.zeros((SUBLANES, sub), BF16) for _ in range(rg_grp)] for _ in range(a_grp)]
                for hh in range(PEER_HEADS):
                    cnt = [jnp.broadcast_to(cnt_b[hh][a0 + k:a0 + k + 1], (SUBLANES, sub)) for k in range(a_grp)]
                    wa = [jnp.broadcast_to(wa_b[hh][a0 + k:a0 + k + 1], (SUBLANES, sub)) for k in range(a_grp)]
                    for r in range(rg_grp):
                        bs = slice((r0 + r) * SUBLANES, (r0 + r + 1) * SUBLANES)
                        rb = rb_ref[hh, bs, ts]
                        wb = wb_ref[hh, bs, ts]
                        for k in range(a_grp):
                            g[k][r] = g[k][r] + jnp.where(rb < cnt[k], wb * wa[k], zero)
                for k in range(a_grp):
                    for r in range(rg_grp):
                        e0 = (a0 + k) * N_KEYS + (r0 + r) * SUBLANES
                        x = s_ref[c % n_slot, e0:e0 + SUBLANES, :]
                        act = x * (1.0 + lax.erf(x * (1.0 / math.sqrt(2.0))))
                        pieces[e0 // SUBLANES] = act.astype(BF16) * g[k][r]
        return jnp.concatenate(pieces, axis=0)

    pre_act(0)
    for c in range(n_sub):
        if c + 1 < n_sub:
            pre_act(c + 1)
        hact = gate_and_activate(c)
        o_ref[:, c * sub:(c + 1) * sub] += lax.dot_general(v_ref[...], hact, (((0,), (0,)), ((), ())),
                                                          preferred_element_type=F32)


def _peer_dense(x1t, u_b, v_b, cnt, wa, rb, wb):
    d, t = x1t.shape
    ne = u_b.shape[0]
    tt = min(TT_DENSE, t)
    sub = min(SUB_DENSE, tt)
    eb = EB_DENSE
    a_rows = eb // N_KEYS
    row_meta = pl.BlockSpec((PEER_HEADS, a_rows, tt), lambda i, j: (0, j, i))
    key_meta = pl.BlockSpec((PEER_HEADS, N_KEYS, tt), lambda i, j: (0, 0, i))
    return pl.pallas_call(
        functools.partial(_peer_dense_kernel, sub=sub),
        grid=(t // tt, ne // eb),
        in_specs=[pl.BlockSpec((d, tt), lambda i, j: (0, i)),
                  pl.BlockSpec((eb, d), lambda i, j: (j, 0)),
                  pl.BlockSpec((eb, d), lambda i, j: (j, 0)),
                  row_meta, row_meta, key_meta, key_meta],
        out_specs=pl.BlockSpec((d, tt), lambda i, j: (0, i)),
        out_shape=jax.ShapeDtypeStruct((d, t), F32),
        scratch_shapes=[pltpu.VMEM((min(2, tt // sub), eb, sub), F32)],
        compiler_params=pltpu.CompilerParams(dimension_semantics=("arbitrary", "arbitrary"),
                                             vmem_limit_bytes=VMEM_LIMIT_DENSE),
        name="peer_dense",
    )(x1t, u_b, v_b, cnt, wa, rb, wb)


def _ffn_ln_kernel(ffnt_ref, x1_ref, g_ref, b_ref, o_ref):
    ffn = ffnt_ref[...].T
    o_ref[...] = _layer_norm_rows(ALPHA * x1_ref[...] + ffn, g_ref[...], b_ref[...])


def _ffn_ln(ffnt, x1, ln_g, ln_b):
    t, d = x1.shape
    tm = min(TM_PROJ, t)
    const = lambda shape: pl.BlockSpec(shape, lambda i: (0,) * len(shape))
    return pl.pallas_call(
        _ffn_ln_kernel,
        grid=(t // tm,),
        in_specs=[pl.BlockSpec((d, tm), lambda i: (0, i)), pl.BlockSpec((tm, d), lambda i: (i, 0)),
                  const((1, d)), const((1, d))],
        out_specs=pl.BlockSpec((tm, d), lambda i: (i, 0)),
        out_shape=jax.ShapeDtypeStruct((t, d), F32),
        compiler_params=pltpu.CompilerParams(dimension_semantics=("arbitrary",),
                                             vmem_limit_bytes=VMEM_LIMIT),
        name="ffn_ln",
    )(ffnt, x1, ln_g, ln_b)


def _alibi_features(slopes2, tk, sq):
    assert tk <= 512 and sq <= 256
    c = jnp.asarray(slopes2, F32)
    nh = c.shape[0]
    hi = c.astype(BF16).astype(F32)
    mid = (c - hi).astype(BF16).astype(F32)
    lo = (c - hi - mid).astype(BF16).astype(F32)
    parts = jnp.stack([hi, mid, lo], axis=1)
    kr = np.arange(tk)
    kfeat = jnp.zeros((nh, tk, FEAT_WIDTH), F32)
    kfeat = kfeat.at[:, :, 0:3].set((kr % 256).astype(np.float32)[None, :, None])
    kfeat = kfeat.at[:, :, 3:6].set((256 * (kr // 256)).astype(np.float32)[None, :, None])
    kfeat = kfeat.at[:, :, 6:9].set(jnp.broadcast_to(parts[:, None, :], (nh, tk, 3)))
    qfeat = jnp.zeros((nh, FEAT_WIDTH, sq), F32)
    qfeat = qfeat.at[:, 0:3, :].set(jnp.broadcast_to(parts[:, :, None], (nh, 3, sq)))
    qfeat = qfeat.at[:, 3:6, :].set(jnp.broadcast_to(parts[:, :, None], (nh, 3, sq)))
    qfeat = qfeat.at[:, 6:9, :].set(jnp.broadcast_to(-jnp.arange(sq, dtype=F32)[None, None, :], (nh, 3, sq)))
    return kfeat.astype(BF16), qfeat.astype(BF16)


def kernel(x, w_in, gm_norm_g, gm_norm_b, gm_w_s, gm_b_s, lam_q1, lam_k1, lam_q2, lam_k2, da_norm_g,
           w_o, ln1_g, ln1_b, peer_w_q, peer_keys_a, peer_keys_b, peer_u, peer_v, ln2_g, ln2_b):
    b, s, d = x.shape
    t = b * s
    gm_width = gm_norm_g.shape[0]
    da_width = da_norm_g.shape[0] * da_norm_g.shape[1]
    dk = lam_q1.shape[0]
    assert s % CHUNK == 0 and gm_w_s.shape == (GM_HEADS, CHUNK, CHUNK)
    assert peer_u.shape[0] == N_KEYS * N_KEYS and peer_keys_a.shape[:2] == (PEER_HEADS, N_KEYS)

    row = lambda v: v.reshape(1, -1).astype(F32)
    x2 = x.reshape(t, d)
    bs_tile = jnp.repeat(gm_b_s.T, gm_width // GM_HEADS, axis=1)
    q_scale = (dk ** -0.5) * LOG2E
    assert s % TK == 0
    slopes2 = jnp.asarray(2.0 ** (-(8.0 / DA_HEADS) * np.arange(1, DA_HEADS + 1)) * LOG2E, F32)
    kfeat, qfeat = _alibi_features(slopes2, TK, SQ)
    ygm, qt, k, vt3 = _proj_gmlp(x2, w_in.astype(BF16), kfeat, row(gm_norm_g), row(gm_norm_b), gm_w_s,
                                 bs_tile, gm_width, da_width, q_scale)

    g_lanes = jnp.broadcast_to(da_norm_g.astype(F32)[:, :, None], da_norm_g.shape + (LANES,))
    yda = _diff_attn(qt, qfeat, k.reshape(b, s, -1), vt3, slopes2, row(lam_q1), row(lam_k1), row(lam_q2),
                     row(lam_k2), g_lanes)

    x1, x1b, x1t = _mix_ln(ygm, yda.reshape(t, da_width), x2, w_o.astype(BF16), row(ln1_g), row(ln1_b))
    cnt, wa, rb, wb, u_b, v_b = _peer_select(x1b, peer_w_q.astype(BF16), peer_keys_a.astype(BF16),
                                             peer_keys_b.astype(BF16), peer_u, peer_v)
    ffnt = _peer_dense(x1t, u_b, v_b, cnt, wa, rb, wb)
    out = _ffn_ln(ffnt, x1, row(ln2_g), row(ln2_b))
    return out.reshape(b, s, d)
```

```python
import functools
import math

import jax
import jax.numpy as jnp
import numpy as np
from jax import lax
from jax.experimental import pallas as pl
from jax.experimental.pallas import tpu as pltpu

F32 = jnp.float32
BF16 = jnp.bfloat16

GM_HEADS = 4
CHUNK = 128
DA_HEADS = 4
LAMBDA_INIT = 0.8 - 0.6 * math.exp(-0.3 * (1 - 1))
PEER_HEADS = 8
N_KEYS = 128
PEER_TOPK = 16
DEPTH = 1
ALPHA = (2.0 * DEPTH) ** 0.25
LN_EPS = 1e-5
NEG_BIG = -1e30
LOG2E = 1.4426950408889634
NOT_RANKED = 64.0

LANES = 128
SUBLANES = 8
VMEM_LIMIT = 48 * 1024 * 1024
VMEM_LIMIT_DENSE = 58 * 1024 * 1024
TM_PROJ = 1024
TQ = 512
SQ = 256
TK = 512
RC_ATTN = 64
ONES_ROWS = 16
FEAT_WIDTH = 128
TL_SEL = 512
TT_DENSE = 1024
SUB_DENSE = 1024
EB_DENSE = 2048


def _gelu_exact(x):
    return 0.5 * x * (1.0 + lax.erf(x * (1.0 / math.sqrt(2.0))))


def _layer_norm_rows(x, g, b):
    mu = jnp.mean(x, axis=-1, keepdims=True)
    xc = x - mu
    var = jnp.mean(xc * xc, axis=-1, keepdims=True)
    return xc * lax.rsqrt(var + LN_EPS) * g + b


def _nt_dot(a, b):
    return lax.dot_general(a, b, (((1,), (1,)), ((), ())), preferred_element_type=F32)


def _proj_gmlp_kernel(x_ref, w_ref, kfeat_ref, g_ref, b_ref, ws_ref, bs_ref,
                      ygm_ref, qt_ref, k_ref, vt_ref, *, gm_width, da_width, q_scale):
    xb = x_ref[...].astype(BF16)
    tm = xb.shape[0]
    hd = gm_width // GM_HEADS
    u = _gelu_exact(jnp.dot(xb, w_ref[:, :gm_width], preferred_element_type=F32))
    z = _gelu_exact(jnp.dot(xb, w_ref[:, gm_width:2 * gm_width], preferred_element_type=F32))
    vn = _layer_norm_rows(z, g_ref[...], b_ref[...]).astype(BF16)
    row = lax.broadcasted_iota(jnp.int32, (CHUNK, CHUNK), 0)
    col = lax.broadcasted_iota(jnp.int32, (CHUNK, CHUNK), 1)
    causal = col <= row
    for hh in range(GM_HEADS):
        w_c = jnp.where(causal, ws_ref[hh], 0.0).astype(BF16)
        for c in range(tm // CHUNK):
            rs = slice(c * CHUNK, (c + 1) * CHUNK)
            cs = slice(hh * hd, (hh + 1) * hd)
            mixed = jnp.dot(w_c, vn[rs, cs], preferred_element_type=F32) + bs_ref[:, cs]
            ygm_ref[rs, cs] = (u[rs, cs] * mixed).astype(BF16)
    o = 2 * gm_width
    dvh = da_width // DA_HEADS
    fw = kfeat_ref.shape[2]
    k = jnp.dot(xb, w_ref[:, o + da_width:o + 2 * da_width], preferred_element_type=F32).astype(BF16)
    tn_t = lambda w, x: lax.dot_general(w, x, (((0,), (1,)), ((), ())), preferred_element_type=F32)
    qt_ref[...] = (tn_t(w_ref[:, o:o + da_width], xb) * q_scale).astype(BF16)
    vt = tn_t(w_ref[:, o + 2 * da_width:o + 3 * da_width], xb).astype(BF16)
    ones = jnp.ones((ONES_ROWS, tm), BF16)
    for hh in range(DA_HEADS):
        k_ref[:, hh * (dvh + fw):hh * (dvh + fw) + dvh] = k[:, hh * dvh:(hh + 1) * dvh]
        k_ref[:, hh * (dvh + fw) + dvh:(hh + 1) * (dvh + fw)] = kfeat_ref[hh]
        r0 = hh * (dvh + ONES_ROWS)
        vt_ref[r0:r0 + dvh, :] = vt[hh * dvh:(hh + 1) * dvh]
        vt_ref[r0 + dvh:r0 + dvh + ONES_ROWS, :] = ones


def _proj_gmlp(x2, w_in_b, kfeat, gm_g, gm_b, w_s, bs_tile, gm_width, da_width, q_scale):
    t, d = x2.shape
    tm = TK
    cols = w_in_b.shape[1]
    kw = da_width + DA_HEADS * kfeat.shape[2]
    vr = da_width + DA_HEADS * ONES_ROWS
    o = 2 * gm_width
    const = lambda shape: pl.BlockSpec(shape, lambda i: (0,) * len(shape))
    row_blk = lambda w: pl.BlockSpec((tm, w), lambda i: (i, 0))
    return pl.pallas_call(
        functools.partial(_proj_gmlp_kernel, gm_width=gm_width, da_width=da_width, q_scale=q_scale),
        grid=(t // tm,),
        in_specs=[row_blk(d), const((d, cols)), const(kfeat.shape),
                  const((1, gm_width)), const((1, gm_width)),
                  const((GM_HEADS, CHUNK, CHUNK)), const((CHUNK, gm_width))],
        out_specs=[row_blk(gm_width), pl.BlockSpec((da_width, tm), lambda i: (0, i)), row_blk(kw),
                   pl.BlockSpec((None, vr, tm), lambda i: (i, 0, 0))],
        out_shape=[jax.ShapeDtypeStruct((t, gm_width), BF16), jax.ShapeDtypeStruct((da_width, t), BF16),
                   jax.ShapeDtypeStruct((t, kw), BF16),
                   jax.ShapeDtypeStruct((t // tm, vr, tm), BF16)],
        compiler_params=pltpu.CompilerParams(dimension_semantics=("arbitrary",),
                                             vmem_limit_bytes=VMEM_LIMIT),
        name="proj_gmlp",
    )(x2, w_in_b, kfeat, gm_g, gm_b, w_s, bs_tile)


def _diff_attn_kernel(slope_ref, qt_ref, qaug_ref, k_ref, vt_ref, lq1_ref, lk1_ref, lq2_ref, lk2_ref, gb_ref, o_ref,
                      s_scr, p_scr, mask_scr, acc_scr, *, tq, tk, sq, rc):
    h = pl.program_id(1)
    qi = pl.program_id(2)
    slope2 = slope_ref[h]
    dv = qt_ref.shape[0]
    dk = dv // 2
    n_sq = tq // sq
    n_ch = 2 * n_sq
    n_full = qi

    @pl.when(qi == 0)
    def _():
        kr = lax.broadcasted_iota(jnp.int32, (tk, sq), 0)
        qc = lax.broadcasted_iota(jnp.int32, (tk, sq), 1)
        for si in range(n_sq):
            mask_scr[si] = jnp.where(kr <= qc + si * sq, 0.0, NEG_BIG)

    rowq = lax.broadcasted_iota(jnp.int32, (dv, sq), 0)
    chains = []
    for si in range(n_sq):
        qt = qt_ref[:, si * sq:(si + 1) * sq]
        zero = jnp.zeros_like(qt)
        chains.append((si, jnp.concatenate([jnp.where(rowq < dk, qt, zero), qaug_ref[...]], axis=0)))
        chains.append((si, jnp.concatenate([jnp.where(rowq >= dk, qt, zero), qaug_ref[...]], axis=0)))

    def logits_to_scratch(kj, slot):
        kb = k_ref[pl.ds(pl.multiple_of(kj * tk, tk), tk), :]
        for ci, (_, qm) in enumerate(chains):
            s_scr[slot, ci] = jnp.dot(kb, qm, preferred_element_type=F32)

    def softmax_step(kj, slot, ml, diag):
        out, alphas = [], []
        for ci, (si, _) in enumerate(chains):
            m = ml[ci]
            rel = kj * tk - (qi * tq + si * sq)
            off = rel.astype(F32) * slope2

            def logits(r0):
                t = s_scr[slot, ci, r0:r0 + rc, :]
                return t + mask_scr[si, r0:r0 + rc, :] if diag else t

            cmax = logits(0)
            for r0 in range(rc, tk, rc):
                cmax = jnp.maximum(cmax, logits(r0))
            m_new = jnp.maximum(m, jnp.max(cmax, axis=0, keepdims=True) + off)
            shift = off - m_new
            for r0 in range(0, tk, rc):
                p_scr[slot, ci, r0:r0 + rc, :] = jnp.exp2(logits(r0) + shift).astype(BF16)
            out.append(m_new)
            alphas.append(jnp.exp2(m - m_new))
        return tuple(out), alphas

    def accumulate_previous(kj_prev, slot_prev, alphas):
        vb = vt_ref[kj_prev]
        for ci in range(n_ch):
            pv = jnp.dot(vb, p_scr[slot_prev, ci], preferred_element_type=F32)
            acc_scr[ci] = alphas[ci] * (acc_scr[ci] + pv)

    def step(kj, slot, ml, prefetch_logits, diag):
        if prefetch_logits:
            logits_to_scratch(kj + 1, 1 - slot)
        ml, alphas = softmax_step(kj, slot, ml, diag)
        accumulate_previous(jnp.maximum(kj - 1, 0), 1 - slot, alphas)
        return ml

    def finalize(kj, slot, ml):
        vb = vt_ref[kj]
        lam = (jnp.exp(jnp.sum(lq1_ref[...] * lk1_ref[...], axis=1, keepdims=True))
               - jnp.exp(jnp.sum(lq2_ref[...] * lk2_ref[...], axis=1, keepdims=True)) + LAMBDA_INIT)
        g = jnp.concatenate([gb_ref[...]] * (sq // gb_ref.shape[1]), axis=1)
        for si in range(n_sq):
            a = [acc_scr[2 * si + mp] + jnp.dot(vb, p_scr[slot, 2 * si + mp], preferred_element_type=F32)
                 for mp in range(2)]
            l1, l2 = a[0][dv:dv + 1], a[1][dv:dv + 1]
            att = a[0][:dv] / l1 - lam * (a[1][:dv] / l2)
            ms = jnp.mean(att * att, axis=0, keepdims=True)
            y = att * lax.rsqrt(ms + LN_EPS) * g
            o_ref[si * sq:(si + 1) * sq, :] = (y * (1.0 - LAMBDA_INIT)).T.astype(o_ref.dtype)

    acc_scr[...] = jnp.zeros_like(acc_scr)
    p_scr[1] = jnp.zeros(p_scr.shape[1:], BF16)
    logits_to_scratch(0, 0)
    ml0 = (jnp.full((1, sq), NEG_BIG, F32),) * n_ch

    def pair(i, ml):
        ml = step(2 * i, 0, ml, True, False)
        return step(2 * i + 1, 1, ml, True, False)

    ml = lax.fori_loop(0, n_full // 2, pair, ml0)

    @pl.when(n_full % 2 == 0)
    def _():
        finalize(n_full, 0, step(n_full, 0, ml, False, True))

    @pl.when(n_full % 2 == 1)
    def _():
        ml1 = step(n_full - 1, 0, ml, True, False)
        finalize(n_full, 1, step(n_full, 1, ml1, False, True))


def _diff_attn(qt, qaug, k3, vt3, slopes2, lq1, lk1, lq2, lk2, gb):
    b, s, _ = k3.shape
    dv = qt.shape[0] // DA_HEADS
    kw = k3.shape[2] // DA_HEADS
    dva = vt3.shape[1] // DA_HEADS
    w = dv * DA_HEADS
    tq, tk, sq = min(TQ, s), min(TK, s), min(SQ, s)
    assert tq == tk
    nq, nkb = s // tq, s // tk
    n_ch = 2 * (tq // sq)
    dk = lq1.shape[1]
    vec = pl.BlockSpec((1, dk), lambda bi, hi, qi: (0, 0))
    return pl.pallas_call(
        functools.partial(_diff_attn_kernel, tq=tq, tk=tk, sq=sq, rc=RC_ATTN),
        scratch_shapes=[pltpu.VMEM((2, n_ch, tk, sq), F32), pltpu.VMEM((2, n_ch, tk, sq), BF16),
                        pltpu.VMEM((tq // sq, tk, sq), F32), pltpu.VMEM((n_ch, dva, sq), F32)],
        grid=(b, DA_HEADS, nq),
        in_specs=[pl.BlockSpec(memory_space=pltpu.SMEM),
                  pl.BlockSpec((dv, tq), lambda bi, hi, qi: (hi, bi * nq + qi)),
                  pl.BlockSpec((None, kw - dv, sq), lambda bi, hi, qi: (hi, 0, 0)),
                  pl.BlockSpec((None, s, kw), lambda bi, hi, qi: (bi, 0, hi)),
                  pl.BlockSpec((nkb, dva, tk), lambda bi, hi, qi: (bi, hi, 0)),
                  vec, vec, vec, vec,
                  pl.BlockSpec((None, dv, LANES), lambda bi, hi, qi: (hi, 0, 0))],
        out_specs=pl.BlockSpec((None, tq, dv), lambda bi, hi, qi: (bi, qi, hi)),
        out_shape=jax.ShapeDtypeStruct((b, s, w), BF16),
        compiler_params=pltpu.CompilerParams(dimension_semantics=("arbitrary",) * 3,
                                             vmem_limit_bytes=VMEM_LIMIT),
        name="diff_attn",
    )(slopes2, qt, qaug, k3, vt3, lq1, lk1, lq2, lk2, gb)


def _mix_ln_kernel(ygm_ref, yda_ref, x_ref, wo_ref, g_ref, b_ref, x1_ref, x1b_ref, x1t_ref):
    gw = ygm_ref.shape[1]
    mix = (jnp.dot(ygm_ref[...], wo_ref[:gw, :], preferred_element_type=F32)
           + jnp.dot(yda_ref[...], wo_ref[gw:, :], preferred_element_type=F32))
    x1 = _layer_norm_rows(ALPHA * x_ref[...] + mix, g_ref[...], b_ref[...])
    x1_ref[...] = x1
    x1b_ref[...] = x1.astype(BF16)
    x1t_ref[...] = x1.T.astype(BF16)


def _mix_ln(ygm, yda, x2, wo_b, ln_g, ln_b):
    t, d = x2.shape
    tm = min(TM_PROJ, t)
    gw, dw = ygm.shape[1], yda.shape[1]
    const = lambda shape: pl.BlockSpec(shape, lambda i: (0,) * len(shape))
    row_blk = lambda w: pl.BlockSpec((tm, w), lambda i: (i, 0))
    return pl.pallas_call(
        _mix_ln_kernel,
        grid=(t // tm,),
        in_specs=[row_blk(gw), row_blk(dw), row_blk(d), const(wo_b.shape), const((1, d)), const((1, d))],
        out_specs=[row_blk(d), row_blk(d), pl.BlockSpec((d, tm), lambda i: (0, i))],
        out_shape=[jax.ShapeDtypeStruct((t, d), F32), jax.ShapeDtypeStruct((t, d), BF16),
                   jax.ShapeDtypeStruct((d, t), BF16)],
        compiler_params=pltpu.CompilerParams(dimension_semantics=("arbitrary",),
                                             vmem_limit_bytes=VMEM_LIMIT),
        name="mix_ln",
    )(ygm, yda, x2, wo_b, ln_g, ln_b)


def _top16_exact(val, rank_ref, vals_ref):
    key_iota = lax.broadcasted_iota(jnp.int32, val.shape, 0).astype(F32)
    rank = jnp.full(val.shape, NOT_RANKED, F32)
    for r in range(PEER_TOPK):
        m = jnp.max(val, axis=0, keepdims=True)
        idx = jnp.min(jnp.where(val == m, key_iota, float(N_KEYS)), axis=0, keepdims=True)
        hit = key_iota == idx
        rank = jnp.where(hit, float(r), rank)
        val = jnp.where(hit, -jnp.inf, val)
        vals_ref[r:r + 1, :] = m
    rank_ref[...] = rank


def _top16_if_distinct(val, rank_ref, vals_ref):
    rank = jnp.full(val.shape, NOT_RANKED, F32)
    for r in range(PEER_TOPK):
        m = jnp.max(val, axis=0, keepdims=True)
        hit = val == m
        rank = jnp.where(hit, float(r), rank)
        val = jnp.where(hit, -jnp.inf, val)
        vals_ref[r:r + 1, :] = m
    rank_ref[...] = rank
    return jnp.sum(jnp.where(val == -jnp.inf, 1.0, 0.0), axis=0, keepdims=True)


_PIECES = 10
_J_LIMIT = (8, 8, 8, 5, 4, 3, 2, 2, 2, 8)


def _candidates(va, vb):
    tl = va.shape[1]
    j8 = lax.broadcasted_iota(jnp.int32, (8, tl), 0).astype(F32)
    pieces, flats = [], []
    for p in range(_PIECES):
        if p == 0:
            c, f = vb[0:8] + va[0:1], j8
        elif p == 1:
            c, f = vb[8:16] + va[0:1], j8 + 8.0
        elif p == 9:
            c, f = va[8:16] + vb[0:1], 128.0 + 16.0 * j8
        else:
            i = p - 1
            c, f = vb[0:8] + va[i:i + 1], j8 + 16.0 * i
            c = jnp.where(j8 < float(_J_LIMIT[p]), c, -jnp.inf)
        pieces.append(c)
        flats.append(f)
    return jnp.concatenate(pieces, axis=0), jnp.concatenate(flats, axis=0)


def _joint_exact(cand, flat, s0, cnt_scr, z_scr):
    tl = cand.shape[1]
    i16 = lax.broadcasted_iota(jnp.int32, (PEER_TOPK, tl), 0).astype(F32)
    cnt = jnp.zeros((PEER_TOPK, tl), F32)
    z = jnp.zeros((1, tl), F32)
    for _ in range(PEER_TOPK):
        m = jnp.max(cand, axis=0, keepdims=True)
        idx = jnp.min(jnp.where(cand == m, flat, 1024.0), axis=0, keepdims=True)
        cand = jnp.where(flat == idx, -jnp.inf, cand)
        cnt = cnt + jnp.where(i16 == jnp.floor(idx * (1.0 / PEER_TOPK)), 1.0, 0.0)
        z = z + jnp.exp(m - s0)
    cnt_scr[...] = cnt
    z_scr[0:1, :] = z


def _joint_if_distinct(cand, s0, cnt_scr, z_scr):
    taken = jnp.zeros(cand.shape, F32)
    z = jnp.zeros((1, cand.shape[1]), F32)
    for _ in range(PEER_TOPK):
        m = jnp.max(cand, axis=0, keepdims=True)
        hit = cand == m
        taken = jnp.where(hit, 1.0, taken)
        cand = jnp.where(hit, -jnp.inf, cand)
        z = z + jnp.exp(m - s0)
    rowsum = lambda lo, hi: jnp.sum(taken[lo:hi], axis=0, keepdims=True)
    cnt_scr[0:1, :] = rowsum(0, 16)
    for i in range(1, 8):
        cnt_scr[i:i + 1, :] = rowsum(8 * (i + 1), 8 * (i + 2))
    cnt_scr[8:16, :] = taken[72:80]
    z_scr[0:1, :] = z
    return jnp.sum(taken, axis=0, keepdims=True)


def _key_scores(x1b, wq, ka, kb):
    half = ka.shape[1]
    qa = jnp.dot(x1b, wq[:, :half], preferred_element_type=F32).astype(BF16)
    qb = jnp.dot(x1b, wq[:, half:], preferred_element_type=F32).astype(BF16)
    return _nt_dot(ka, qa), _nt_dot(kb, qb)


def _peer_select_kernel(x1f_ref, wqf_ref, kaf_ref, kbf_ref, x1n_ref, wqn_ref, kan_ref, kbn_ref, u_ref, v_ref,
                        cnt_ref, wa_ref, rb_ref, wb_ref, ub_ref, vb16_ref,
                        sc_scr, va_ref, vb_ref, ra_scr, rb_scr, cnt_scr, z_scr):
    ub_ref[...] = u_ref[...].astype(BF16)
    vb16_ref[...] = v_ref[...].astype(BF16)
    i = pl.program_id(0)
    h = pl.program_id(1)
    slot = h % 2

    @pl.when((i == 0) & (h == 0))
    def _():
        sa0, sb0 = _key_scores(x1f_ref[...], wqf_ref[...], kaf_ref[...], kbf_ref[...])
        sc_scr[0, 0] = sa0
        sc_scr[0, 1] = sb0

    sa = sc_scr[slot, 0]
    sb = sc_scr[slot, 1]
    sa_next, sb_next = _key_scores(x1n_ref[...], wqn_ref[...], kan_ref[...], kbn_ref[...])
    sc_scr[1 - slot, 0] = sa_next
    sc_scr[1 - slot, 1] = sb_next

    ea = _top16_if_distinct(sa, ra_scr, va_ref)
    eb = _top16_if_distinct(sb, rb_scr, vb_ref)
    va = va_ref[...]
    vb = vb_ref[...]
    cand, _ = _candidates(va, vb)
    ec = _joint_if_distinct(cand, va[0:1] + vb[0:1], cnt_scr, z_scr)
    extracted = jnp.maximum(jnp.maximum(ea, eb), ec)

    @pl.when(jnp.max(extracted) > float(PEER_TOPK))
    def _():
        _top16_exact(sa, ra_scr, va_ref)
        _top16_exact(sb, rb_scr, vb_ref)
        va_x = va_ref[...]
        vb_x = vb_ref[...]
        cand_x, flat_x = _candidates(va_x, vb_x)
        _joint_exact(cand_x, flat_x, va_x[0:1] + vb_x[0:1], cnt_scr, z_scr)

    ra = ra_scr[...].astype(BF16)
    cnt = cnt_scr[...].astype(BF16)
    cnt_key = jnp.zeros(ra.shape, BF16)
    for r in range(PEER_TOPK):
        cnt_key = jnp.where(ra == float(r), cnt[r:r + 1], cnt_key)
    cnt_ref[...] = cnt_key.astype(F32)
    z = z_scr[0:1, :]
    wa_ref[...] = jnp.exp(sa - va_ref[0:1, :]) * (0.5 / z)
    rb_ref[...] = rb_scr[...].astype(BF16)
    wb_ref[...] = jnp.exp(sb - vb_ref[0:1, :]).astype(BF16)


def _peer_select(x1b, wq_b, ka_b, kb_b, u_tab, v_tab):
    t, d = x1b.shape
    nh, nk, half = ka_b.shape
    assert nh % 2 == 0
    tl = min(TL_SEL, t)
    nt = t // tl
    ne, dt = u_tab.shape
    slab = ne // (nt * nh)
    assert slab * nt * nh == ne and slab % 16 == 0
    tab_blk = pl.BlockSpec((slab, dt), lambda i, h: (i * nh + h, 0))
    tab_b16 = jax.ShapeDtypeStruct((ne, dt), BF16)
    qcols = 2 * half
    out_blk = pl.BlockSpec((None, nk, tl), lambda i, h: (h, 0, i))
    out_f32 = jax.ShapeDtypeStruct((nh, nk, t), F32)
    out_b16 = jax.ShapeDtypeStruct((nh, nk, t), BF16)
    first = [pl.BlockSpec((tl, d), lambda i, h: (0, 0)), pl.BlockSpec((d, qcols), lambda i, h: (0, 0)),
             pl.BlockSpec((None, nk, half), lambda i, h: (0, 0, 0)),
             pl.BlockSpec((None, nk, half), lambda i, h: (0, 0, 0))]
    nxt = [pl.BlockSpec((tl, d), lambda i, h: (jnp.minimum(i + (h + 1) // nh, nt - 1), 0)),
           pl.BlockSpec((d, qcols), lambda i, h: (0, (h + 1) % nh)),
           pl.BlockSpec((None, nk, half), lambda i, h: ((h + 1) % nh, 0, 0)),
           pl.BlockSpec((None, nk, half), lambda i, h: ((h + 1) % nh, 0, 0))]
    return pl.pallas_call(
        _peer_select_kernel,
        grid=(nt, nh),
        in_specs=first + nxt + [tab_blk, tab_blk],
        out_specs=[out_blk, out_blk, out_blk, out_blk, tab_blk, tab_blk],
        out_shape=[out_f32, out_f32, out_b16, out_b16, tab_b16, tab_b16],
        scratch_shapes=[pltpu.VMEM((2, 2, nk, tl), F32),
                        pltpu.VMEM((PEER_TOPK, tl), F32), pltpu.VMEM((PEER_TOPK, tl), F32),
                        pltpu.VMEM((nk, tl), F32), pltpu.VMEM((nk, tl), F32),
                        pltpu.VMEM((PEER_TOPK, tl), F32), pltpu.VMEM((SUBLANES, tl), F32)],
        compiler_params=pltpu.CompilerParams(dimension_semantics=("arbitrary", "arbitrary"),
                                             vmem_limit_bytes=VMEM_LIMIT),
        name="peer_select",
    )(x1b, wq_b, ka_b, kb_b, x1b, wq_b, ka_b, kb_b, u_tab, v_tab)


def _peer_dense_kernel(x1t_ref, u_ref, v_ref, cnt_ref, wa_ref, rb_ref, wb_ref, o_ref, s_ref, *, sub):
    j = pl.program_id(1)
    eb = u_ref.shape[0]
    tt = x1t_ref.shape[1]
    n_sub = tt // sub
    n_slot = s_ref.shape[0]
    a_rows = eb // N_KEYS
    a_grp = 2
    n_rg = N_KEYS // SUBLANES
    rg_grp = 8
    zero = jnp.zeros((), BF16)

    @pl.when(j == 0)
    def _():
        o_ref[...] = jnp.zeros_like(o_ref)

    def pre_act(c):
        s_ref[c % n_slot] = jnp.dot(u_ref[...], x1t_ref[:, c * sub:(c + 1) * sub], preferred_element_type=F32)

    def gate_and_activate(c):
        ts = slice(c * sub, (c + 1) * sub)
        cnt_b = [cnt_ref[hh, :, ts].astype(BF16) for hh in range(PEER_HEADS)]
        wa_b = [wa_ref[hh, :, ts].astype(BF16) for hh in range(PEER_HEADS)]
        pieces = [None] * (eb // SUBLANES)
        for a0 in range(0, a_rows, a_grp):
            for r0 in range(0, n_rg, rg_grp):
                g = [[jnp.zeros((SUBLANES, sub), BF16) for _ in range(rg_grp)] for _ in range(a_grp)]
                for hh in range(PEER_HEADS):
                    cnt = [jnp.broadcast_to(cnt_b[hh][a0 + k:a0 + k + 1], (SUBLANES, sub)) for k in range(a_grp)]
                    wa = [jnp.broadcast_to(wa_b[hh][a0 + k:a0 + k + 1], (SUBLANES, sub)) for k in range(a_grp)]
                    for r in range(rg_grp):
                        bs = slice((r0 + r) * SUBLANES, (r0 + r + 1) * SUBLANES)
                        rb = rb_ref[hh, bs, ts]
                        wb = wb_ref[hh, bs, ts]
                        for k in range(a_grp):
                            g[k][r] = g[k][r] + jnp.where(rb < cnt[k], wb * wa[k], zero)
                for k in range(a_grp):
                    for r in range(rg_grp):
                        e0 = (a0 + k) * N_KEYS + (r0 + r) * SUBLANES
                        x = s_ref[c % n_slot, e0:e0 + SUBLANES, :]
                        act = x * (1.0 + lax.erf(x * (1.0 / math.sqrt(2.0))))
                        pieces[e0 // SUBLANES] = act.astype(BF16) * g[k][r]
        return jnp.concatenate(pieces, axis=0)

    pre_act(0)
    for c in range(n_sub):
        if c + 1 < n_sub:
            pre_act(c + 1)
        hact = gate_and_activate(c)
        o_ref[:, c * sub:(c + 1) * sub] += lax.dot_general(v_ref[...], hact, (((0,), (0,)), ((), ())),
                                                          preferred_element_type=F32)


def _peer_dense(x1t, u_b, v_b, cnt, wa, rb, wb):
    d, t = x1t.shape
    ne = u_b.shape[0]
    tt = min(TT_DENSE, t)
    sub = min(SUB_DENSE, tt)
    eb = EB_DENSE
    a_rows = eb // N_KEYS
    row_meta = pl.BlockSpec((PEER_HEADS, a_rows, tt), lambda i, j: (0, j, i))
    key_meta = pl.BlockSpec((PEER_HEADS, N_KEYS, tt), lambda i, j: (0, 0, i))
    return pl.pallas_call(
        functools.partial(_peer_dense_kernel, sub=sub),
        grid=(t // tt, ne // eb),
        in_specs=[pl.BlockSpec((d, tt), lambda i, j: (0, i)),
                  pl.BlockSpec((eb, d), lambda i, j: (j, 0)),
                  pl.BlockSpec((eb, d), lambda i, j: (j, 0)),
                  row_meta, row_meta, key_meta, key_meta],
        out_specs=pl.BlockSpec((d, tt), lambda i, j: (0, i)),
        out_shape=jax.ShapeDtypeStruct((d, t), F32),
        scratch_shapes=[pltpu.VMEM((min(2, tt // sub), eb, sub), F32)],
        compiler_params=pltpu.CompilerParams(dimension_semantics=("arbitrary", "arbitrary"),
                                             vmem_limit_bytes=VMEM_LIMIT_DENSE),
        name="peer_dense",
    )(x1t, u_b, v_b, cnt, wa, rb, wb)


def _ffn_ln_kernel(ffnt_ref, x1_ref, g_ref, b_ref, o_ref):
    ffn = ffnt_ref[...].T
    o_ref[...] = _layer_norm_rows(ALPHA * x1_ref[...] + ffn, g_ref[...], b_ref[...])


def _ffn_ln(ffnt, x1, ln_g, ln_b):
    t, d = x1.shape
    tm = min(TM_PROJ, t)
    const = lambda shape: pl.BlockSpec(shape, lambda i: (0,) * len(shape))
    return pl.pallas_call(
        _ffn_ln_kernel,
        grid=(t // tm,),
        in_specs=[pl.BlockSpec((d, tm), lambda i: (0, i)), pl.BlockSpec((tm, d), lambda i: (i, 0)),
                  const((1, d)), const((1, d))],
        out_specs=pl.BlockSpec((tm, d), lambda i: (i, 0)),
        out_shape=jax.ShapeDtypeStruct((t, d), F32),
        compiler_params=pltpu.CompilerParams(dimension_semantics=("arbitrary",),
                                             vmem_limit_bytes=VMEM_LIMIT),
        name="ffn_ln",
    )(ffnt, x1, ln_g, ln_b)


def _alibi_features(slopes2, tk, sq):
    assert tk <= 512 and sq <= 256
    c = jnp.asarray(slopes2, F32)
    nh = c.shape[0]
    hi = c.astype(BF16).astype(F32)
    mid = (c - hi).astype(BF16).astype(F32)
    lo = (c - hi - mid).astype(BF16).astype(F32)
    parts = jnp.stack([hi, mid, lo], axis=1)
    kr = np.arange(tk)
    kfeat = jnp.zeros((nh, tk, FEAT_WIDTH), F32)
    kfeat = kfeat.at[:, :, 0:3].set((kr % 256).astype(np.float32)[None, :, None])
    kfeat = kfeat.at[:, :, 3:6].set((256 * (kr // 256)).astype(np.float32)[None, :, None])
    kfeat = kfeat.at[:, :, 6:9].set(jnp.broadcast_to(parts[:, None, :], (nh, tk, 3)))
    qfeat = jnp.zeros((nh, FEAT_WIDTH, sq), F32)
    qfeat = qfeat.at[:, 0:3, :].set(jnp.broadcast_to(parts[:, :, None], (nh, 3, sq)))
    qfeat = qfeat.at[:, 3:6, :].set(jnp.broadcast_to(parts[:, :, None], (nh, 3, sq)))
    qfeat = qfeat.at[:, 6:9, :].set(jnp.broadcast_to(-jnp.arange(sq, dtype=F32)[None, None, :], (nh, 3, sq)))
    return kfeat.astype(BF16), qfeat.astype(BF16)


def kernel(x, w_in, gm_norm_g, gm_norm_b, gm_w_s, gm_b_s, lam_q1, lam_k1, lam_q2, lam_k2, da_norm_g,
           w_o, ln1_g, ln1_b, peer_w_q, peer_keys_a, peer_keys_b, peer_u, peer_v, ln2_g, ln2_b):
    b, s, d = x.shape
    t = b * s
    gm_width = gm_norm_g.shape[0]
    da_width = da_norm_g.shape[0] * da_norm_g.shape[1]
    dk = lam_q1.shape[0]
    assert s % CHUNK == 0 and gm_w_s.shape == (GM_HEADS, CHUNK, CHUNK)
    assert peer_u.shape[0] == N_KEYS * N_KEYS and peer_keys_a.shape[:2] == (PEER_HEADS, N_KEYS)

    row = lambda v: v.reshape(1, -1).astype(F32)
    x2 = x.reshape(t, d)
    bs_tile = jnp.repeat(gm_b_s.T, gm_width // GM_HEADS, axis=1)
    q_scale = (dk ** -0.5) * LOG2E
    assert s % TK == 0
    slopes2 = jnp.asarray(2.0 ** (-(8.0 / DA_HEADS) * np.arange(1, DA_HEADS + 1)) * LOG2E, F32)
    kfeat, qfeat = _alibi_features(slopes2, TK, SQ)
    ygm, qt, k, vt3 = _proj_gmlp(x2, w_in.astype(BF16), kfeat, row(gm_norm_g), row(gm_norm_b), gm_w_s,
                                 bs_tile, gm_width, da_width, q_scale)

    g_lanes = jnp.broadcast_to(da_norm_g.astype(F32)[:, :, None], da_norm_g.shape + (LANES,))
    yda = _diff_attn(qt, qfeat, k.reshape(b, s, -1), vt3, slopes2, row(lam_q1), row(lam_k1), row(lam_q2),
                     row(lam_k2), g_lanes)

    x1, x1b, x1t = _mix_ln(ygm, yda.reshape(t, da_width), x2, w_o.astype(BF16), row(ln1_g), row(ln1_b))
    cnt, wa, rb, wb, u_b, v_b = _peer_select(x1b, peer_w_q.astype(BF16), peer_keys_a.astype(BF16),
                                             peer_keys_b.astype(BF16), peer_u, peer_v)
    ffnt = _peer_dense(x1t, u_b, v_b, cnt, wa, rb, wb)
    out = _ffn_ln(ffnt, x1, row(ln2_g), row(ln2_b))
    return out.reshape(b, s, d)
```
